```python
import math
import jax
import jax.numpy as jnp
from jax import lax
import numpy as np

D_MODEL = 1024
BATCH = 8
SEQ = 2048
DEPTH = 4

GRID_W = 64
CTX_LEN = 256
EPS = 1e-6
N_MOD = 6
F32 = jnp.float32

GLA_HEADS = 4
GLA_DK = 128
GLA_DV = 256
GLA_QK = GLA_HEADS * GLA_DK
GLA_V = GLA_HEADS * GLA_DV
GLA_RANK = 16
GLA_TAU = 16.0
GLA_CHUNK = 64

FNET_GROUPS = 4
FNET_WIDTH = D_MODEL
FNET_GROUP_DIM = FNET_WIDTH // FNET_GROUPS

SSD_INNER = 2 * D_MODEL
SSD_HEADDIM = 64
SSD_HEADS = SSD_INNER // SSD_HEADDIM
SSD_GROUPS = 4
SSD_STATE = 128
SSD_CONV = 5
SSD_CONV_DIM = SSD_INNER + 2 * SSD_GROUPS * SSD_STATE
SSD_CHUNK = 64

N_EXPERTS = 16
N_EXPERT_GROUPS = 4
EXPERTS_PER_GROUP = N_EXPERTS // N_EXPERT_GROUPS
TOP_K = 2
D_FF_EXPERT = 512

PROJ_SIZES = (
    GLA_QK,
    GLA_QK,
    GLA_V,
    GLA_V,
    GLA_RANK,
    GLA_RANK,
    FNET_WIDTH,
    SSD_INNER,
    SSD_CONV_DIM,
    SSD_HEADS,
    SSD_HEADS,
    D_MODEL,
    D_MODEL,
    D_MODEL,
)
D_IN_PROJ = sum(PROJ_SIZES)

kernel_name = 'hybrid_gla_fnet_ssd_moe_diffusion_trunk'


def rms_norm(x, g):
    xf = x.astype(F32)
    y = xf * lax.rsqrt(jnp.mean(xf * xf, axis=-1, keepdims=True) + EPS)
    return (y * g.astype(F32)).astype(x.dtype)


def modulate(x, g, shift, scale):
    return rms_norm(x, g) * (1.0 + scale) + shift


def grid_sincos(n_tokens, dim):
    rows = n_tokens // GRID_W
    row = jnp.broadcast_to(jnp.arange(rows)[:, None], (rows, GRID_W)).reshape(-1).astype(F32)
    col = jnp.broadcast_to(jnp.arange(GRID_W)[None, :], (rows, GRID_W)).reshape(-1).astype(F32)
    quarter = dim // 4
    freqs = jnp.exp(-math.log(10000.0) * jnp.arange(quarter, dtype=F32) / quarter)
    ar = row[:, None] * freqs
    ac = col[:, None] * freqs
    return jnp.concatenate([jnp.sin(ar), jnp.cos(ar), jnp.sin(ac), jnp.cos(ac)], axis=-1)


def project(h, w_in):
    parts, start = [], 0
    for size in PROJ_SIZES:
        parts.append(h @ w_in[:, start:start + size])
        start += size
    return parts


def gla_prepare(q, k, v, a_lr_f, a_lr_b, a_w, a_b):
    B, T, _ = q.shape

    def heads(t, d):
        return t.reshape(B, T, GLA_HEADS, d).transpose(0, 2, 1, 3)

    q = heads(q, GLA_DK) * (GLA_DK ** -0.5)
    k = heads(k, GLA_DK)
    v = heads(v, GLA_DV)
    la_f = heads(jax.nn.log_sigmoid((a_lr_f @ a_w[0] + a_b[0]).astype(F32)) / GLA_TAU, GLA_DK)
    la_b = heads(jax.nn.log_sigmoid((a_lr_b @ a_w[1] + a_b[1]).astype(F32)) / GLA_TAU, GLA_DK)
    return q, k, v, la_f, la_b


def gla_chunked(q, k, v, log_a, s0):
    B, H, T, dk = q.shape
    dv = v.shape[-1]
    L = GLA_CHUNK
    n = T // L
    q = q.astype(F32).reshape(B, H, n, L, dk)
    k = k.astype(F32).reshape(B, H, n, L, dk)
    v = v.astype(F32).reshape(B, H, n, L, dv)
    b = jnp.cumsum(log_a.astype(F32).reshape(B, H, n, L, dk), axis=3)
    b_ref = b[:, :, :, L // 2 - 1:L // 2]
    b_last = b[:, :, :, L - 1:]
    causal = jnp.tril(jnp.ones((L, L), dtype=bool))
    scores = jnp.einsum('bhnld,bhnsd->bhnls', q * jnp.exp(b - b_ref), k * jnp.exp(b_ref - b))
    scores = jnp.where(causal, scores, 0.0)
    o_intra = jnp.einsum('bhnls,bhnsv->bhnlv', scores, v)
    chunk_kv = jnp.einsum('bhnsd,bhnsv->bhndv', k * jnp.exp(b_last - b), v)
    chunk_decay = jnp.exp(b_last[:, :, :, 0])

    def step(s, inp):
        dec, kv = inp
        return dec[..., None] * s + kv, s

    s_final, s_prev = lax.scan(step, s0.astype(F32),
                               (jnp.moveaxis(chunk_decay, 2, 0), jnp.moveaxis(chunk_kv, 2, 0)))
    s_prev = jnp.moveaxis(s_prev, 0, 2)
    o_inter = jnp.einsum('bhnld,bhndv->bhnlv', q * jnp.exp(b), s_prev)
    return (o_intra + o_inter).reshape(B, H, T, dv), s_final


def gla_output(o, r, norm_g, proj):
    B, H, T, dv = o.shape
    o = rms_norm(o, norm_g).transpose(0, 2, 1, 3).reshape(B, T, GLA_V).astype(r.dtype)
    return (o * jax.nn.silu(r)) @ proj


def fourier_mix(u):
    B, T, _ = u.shape
    ug = u.astype(F32).reshape(B, T, FNET_GROUPS, FNET_GROUP_DIM)
    y = jnp.fft.fft2(ug, axes=(1, 3), norm='ortho').real
    return y.reshape(B, T, FNET_WIDTH).astype(u.dtype)


def centred_dwconv(x, w, b):
    pad = w.shape[0] // 2
    y = lax.conv_general_dilated(x, w[:, None, :], window_strides=(1,), padding=((pad, pad),),
                                 dimension_numbers=('NWC', 'WIO', 'NWC'), feature_group_count=x.shape[-1])
    return y + b


def ssd_prepare(xbc, dtf_raw, dtb_raw, conv_w, conv_b, dt_bias):
    B, T, _ = xbc.shape
    xbc = jax.nn.silu(centred_dwconv(xbc, conv_w, conv_b))
    gn = SSD_GROUPS * SSD_STATE
    xs = xbc[..., :SSD_INNER].reshape(B, T, SSD_HEADS, SSD_HEADDIM)
    bm = xbc[..., SSD_INNER:SSD_INNER + gn].reshape(B, T, SSD_GROUPS, SSD_STATE)
    cm = xbc[..., SSD_INNER + gn:].reshape(B, T, SSD_GROUPS, SSD_STATE)
    dt_f = jax.nn.softplus(dtf_raw.astype(F32) + dt_bias[0].astype(F32))
    dt_b = jax.nn.softplus(dtb_raw.astype(F32) + dt_bias[1].astype(F32))
    return xs, bm, cm, dt_f, dt_b


def ssd_chunked(x, dt, a, bm, cm, s0):
    B, T, H, P = x.shape
    G, N = bm.shape[2], bm.shape[3]
    J = H // G
    L = SSD_CHUNK
    n = T // L
    xdt = (x.astype(F32) * dt[..., None]).reshape(B, n, L, G, J, P)
    a_cs = jnp.cumsum((dt * a).reshape(B, n, L, G, J), axis=2).transpose(0, 1, 3, 4, 2)
    bm = bm.astype(F32).reshape(B, n, L, G, N)
    cm = cm.astype(F32).reshape(B, n, L, G, N)
    causal = jnp.tril(jnp.ones((L, L), dtype=bool))
    seg = a_cs[..., :, None] - a_cs[..., None, :]
    decay = jnp.exp(jnp.where(causal, seg, -jnp.inf))
    cb = jnp.einsum('bclgn,bcsgn->bcgls', cm, bm)
    y_diag = jnp.einsum('bcgls,bcgjls,bcsgjp->bclgjp', cb, decay, xdt)
    decay_to_end = jnp.exp(a_cs[..., -1:] - a_cs)
    states = jnp.einsum('bcsgn,bcgjs,bcsgjp->bcgjpn', bm, decay_to_end, xdt)
    chunk_decay = jnp.exp(a_cs[..., -1])

    def step(s, inp):
        dec, st = inp
        return dec[..., None, None] * s + st, s

    s_final, s_prev = lax.scan(step, s0.astype(F32).reshape(B, G, J, P, N),
                               (jnp.moveaxis(chunk_decay, 1, 0), jnp.moveaxis(states, 1, 0)))
    s_prev = jnp.moveaxis(s_prev, 0, 1)
    y_off = jnp.einsum('bclgn,bcgjpn,bcgjl->bclgjp', cm, s_prev, jnp.exp(a_cs))
    y = (y_diag + y_off).reshape(B, T, H, P)
    return y, s_final.reshape(B, H, P, N)


def ssd_output(y, xs, z, d_skip, norm_g, proj):
    B, T = z.shape[:2]
    y = y + d_skip.astype(F32)[:, None] * xs.astype(F32)
    y = y.reshape(B, T, SSD_INNER).astype(z.dtype)
    return rms_norm(y * jax.nn.silu(z), norm_g) @ proj


def merge_branches(g_a, g_b, g_c, y_a, y_b, y_c, w_out):
    y = jax.nn.sigmoid(g_a) * y_a + jax.nn.sigmoid(g_b) * y_b + jax.nn.sigmoid(g_c) * y_c
    return y @ w_out


def token_mixers(hc, hl, ctx_out, w_in, gla_a_w, gla_a_b, gla_norm_g, gla_proj, fnet_proj,
                 ssd_conv_w, ssd_conv_b, ssd_dt_bias, ssd_a_log, ssd_d, ssd_norm_g, ssd_proj, w_out):
    B = hl.shape[0]
    pc = project(hc, w_in)
    pl = project(hl, w_in)

    gc = gla_prepare(pc[0], pc[1], pc[2], pc[4], pc[5], gla_a_w, gla_a_b)
    gl = gla_prepare(pl[0], pl[1], pl[2], pl[4], pl[5], gla_a_w, gla_a_b)
    zero_gla = jnp.zeros((B, GLA_HEADS, GLA_DK, GLA_DV), F32)
    oc_f, gst_f = gla_chunked(gc[0], gc[1], gc[2], gc[3], zero_gla)
    oc_b, gst_b = gla_chunked(jnp.flip(gc[0], 2), jnp.flip(gc[1], 2), jnp.flip(gc[2], 2),
                              jnp.flip(gc[4], 2), zero_gla)
    ol_f, _ = gla_chunked(gl[0], gl[1], gl[2], gl[3], gst_f)
    ol_b, _ = gla_chunked(jnp.flip(gl[0], 2), jnp.flip(gl[1], 2), jnp.flip(gl[2], 2),
                          jnp.flip(gl[4], 2), gst_b)
    ya_l = gla_output(ol_f + jnp.flip(ol_b, 2), pl[3], gla_norm_g, gla_proj)

    yb_l = fourier_mix(pl[6]) @ fnet_proj

    a_f = -jnp.exp(ssd_a_log[0].astype(F32))
    a_b = -jnp.exp(ssd_a_log[1].astype(F32))
    sc = ssd_prepare(pc[8], pc[9], pc[10], ssd_conv_w, ssd_conv_b, ssd_dt_bias)
    sl = ssd_prepare(pl[8], pl[9], pl[10], ssd_conv_w, ssd_conv_b, ssd_dt_bias)
    zero_ssd = jnp.zeros((B, SSD_HEADS, SSD_HEADDIM, SSD_STATE), F32)
    yc_f, sst_f = ssd_chunked(sc[0], sc[3], a_f, sc[1], sc[2], zero_ssd)
    yc_b, sst_b = ssd_chunked(jnp.flip(sc[0], 1), jnp.flip(sc[4], 1), a_b,
                              jnp.flip(sc[1], 1), jnp.flip(sc[2], 1), zero_ssd)
    yl_f, _ = ssd_chunked(sl[0], sl[3], a_f, sl[1], sl[2], sst_f)
    yl_b, _ = ssd_chunked(jnp.flip(sl[0], 1), jnp.flip(sl[4], 1), a_b,
                          jnp.flip(sl[1], 1), jnp.flip(sl[2], 1), sst_b)
    yc_l = ssd_output(yl_f + jnp.flip(yl_b, 1), sl[0], pl[7], ssd_d, ssd_norm_g, ssd_proj)

    out_l = merge_branches(pl[11], pl[12], pl[13], ya_l, yb_l, yc_l, w_out)
    if not ctx_out:
        return None, out_l
    ya_c = gla_output(oc_f + jnp.flip(oc_b, 2), pc[3], gla_norm_g, gla_proj)
    yb_c = fourier_mix(pc[6]) @ fnet_proj
    yc_c = ssd_output(yc_f + jnp.flip(yc_b, 1), sc[0], pc[7], ssd_d, ssd_norm_g, ssd_proj)
    out_c = merge_branches(pc[11], pc[12], pc[13], ya_c, yb_c, yc_c, w_out)
    return out_c, out_l


def routed_moe(h, router_w, router_b, w1, w3, w2):
    scores = jax.nn.sigmoid((h @ router_w).astype(F32))
    sel = (scores + router_b.astype(F32)).reshape(-1, N_EXPERT_GROUPS, EXPERTS_PER_GROUP)
    grp_score = jnp.sum(lax.top_k(sel, TOP_K)[0], axis=-1)
    best = jnp.argmax(grp_score, axis=-1)
    in_group = jnp.arange(N_EXPERT_GROUPS)[None, :, None] == best[:, None, None]
    masked = jnp.where(in_group, sel, -jnp.inf).reshape(-1, N_EXPERTS)
    _, idx = lax.top_k(masked, TOP_K)
    w = jnp.take_along_axis(scores, idx, axis=-1)
    w = w / jnp.sum(w, axis=-1, keepdims=True)
    gate = jnp.sum(jax.nn.one_hot(idx, N_EXPERTS, dtype=F32) * w[..., None], axis=1).astype(h.dtype)
    out = jnp.zeros_like(h)
    for e in range(N_EXPERTS):
        he = jax.nn.silu(h @ w1[e]) * (h @ w3[e])
        out = out + gate[:, e:e + 1] * (he @ w2[e])
    return out


def setup_inputs(seed: int = 0) -> dict:
    key = jax.random.key(seed)
    keys = jax.random.split(key, 32)
    counter = [0]

    def nk():
        counter[0] += 1
        return keys[counter[0] - 1]

    def nrm(shape, scale):
        return scale * jax.random.normal(nk(), shape, F32)

    def gain(shape):
        return 1.0 + 0.1 * jax.random.normal(nk(), shape, F32)

    D = D_MODEL
    dt0 = jnp.exp(jax.random.uniform(nk(), (DEPTH, 2, SSD_HEADS), F32,
                                     minval=math.log(1e-3), maxval=math.log(1e-1)))
    return {
        'x': nrm((BATCH, SEQ, D), 1.0),
        'c': nrm((BATCH, D), 1.0),
        'ctx': nrm((BATCH, CTX_LEN, D), 1.0),
        'c_ctx': nrm((D,), 1.0),
        'ada_w': nrm((DEPTH, D, N_MOD * D), 0.5 * D ** -0.5),
        'ada_b': nrm((DEPTH, N_MOD * D), 0.01),
        'norm1_g': gain((DEPTH, D)),
        'norm2_g': gain((DEPTH, D)),
        'w_in': nrm((DEPTH, D, D_IN_PROJ), D ** -0.5),
        'gla_a_w': nrm((DEPTH, 2, GLA_RANK, GLA_QK), GLA_RANK ** -0.5),
        'gla_a_b': nrm((DEPTH, 2, GLA_QK), 0.1),
        'gla_norm_g': gain((DEPTH, GLA_DV)),
        'gla_proj': nrm((DEPTH, GLA_V, D), GLA_V ** -0.5),
        'fnet_proj': nrm((DEPTH, FNET_WIDTH, D), FNET_WIDTH ** -0.5),
        'ssd_conv_w': nrm((DEPTH, SSD_CONV, SSD_CONV_DIM), SSD_CONV ** -0.5),
        'ssd_conv_b': nrm((DEPTH, SSD_CONV_DIM), 0.01),
        'ssd_dt_bias': dt0 + jnp.log(-jnp.expm1(-dt0)),
        'ssd_a_log': jnp.log(jax.random.uniform(nk(), (DEPTH, 2, SSD_HEADS), F32, minval=1.0, maxval=16.0)),
        'ssd_d': gain((DEPTH, SSD_HEADS)),
        'ssd_norm_g': gain((DEPTH, SSD_INNER)),
        'ssd_proj': nrm((DEPTH, SSD_INNER, D), SSD_INNER ** -0.5),
        'w_out': nrm((DEPTH, D, D), D ** -0.5),
        'router_w': nrm((D, N_EXPERTS), D ** -0.5),
        'router_b': nrm((N_EXPERTS,), 0.01),
        'exp_w1': nrm((DEPTH, N_EXPERTS, D, D_FF_EXPERT), D ** -0.5),
        'exp_w3': nrm((DEPTH, N_EXPERTS, D, D_FF_EXPERT), D ** -0.5),
        'exp_w2': nrm((DEPTH, N_EXPERTS, D_FF_EXPERT, D), D_FF_EXPERT ** -0.5),
        'final_g': gain((D,)),
    }


def reference(x, c, ctx, c_ctx, ada_w, ada_b, norm1_g, norm2_g, w_in, gla_a_w, gla_a_b, gla_norm_g,
              gla_proj, fnet_proj, ssd_conv_w, ssd_conv_b, ssd_dt_bias, ssd_a_log, ssd_d, ssd_norm_g,
              ssd_proj, w_out, router_w, router_b, exp_w1, exp_w3, exp_w2, final_g):
    B, T, D = x.shape
    xl = x + grid_sincos(T, D).astype(x.dtype)
    xc = ctx
    c_act = jax.nn.silu(c)
    cc_act = jax.nn.silu(c_ctx)[None]
    for l in range(DEPTH):
        ctx_out = l < DEPTH - 1
        mod_l = jnp.split((c_act @ ada_w[l] + ada_b[l])[:, None, :], N_MOD, axis=-1)
        mod_c = jnp.split((cc_act @ ada_w[l] + ada_b[l])[:, None, :], N_MOD, axis=-1)
        hl = modulate(xl, norm1_g[l], mod_l[0], mod_l[1])
        hc = modulate(xc, norm1_g[l], mod_c[0], mod_c[1])
        yc, yl = token_mixers(hc, hl, ctx_out, w_in[l], gla_a_w[l], gla_a_b[l], gla_norm_g[l], gla_proj[l],
                              fnet_proj[l], ssd_conv_w[l], ssd_conv_b[l], ssd_dt_bias[l], ssd_a_log[l],
                              ssd_d[l], ssd_norm_g[l], ssd_proj[l], w_out[l])
        xl = xl + mod_l[2] * yl
        hl = modulate(xl, norm2_g[l], mod_l[3], mod_l[4])
        if ctx_out:
            xc = xc + mod_c[2] * yc
            hc = modulate(xc, norm2_g[l], mod_c[3], mod_c[4])
            n_c = hc.shape[0] * hc.shape[1]
            y = routed_moe(jnp.concatenate([hc.reshape(-1, D), hl.reshape(-1, D)], axis=0),
                           router_w, router_b, exp_w1[l], exp_w3[l], exp_w2[l])
            xc = xc + mod_c[5] * y[:n_c].reshape(hc.shape)
            xl = xl + mod_l[5] * y[n_c:].reshape(hl.shape)
        else:
            y = routed_moe(hl.reshape(-1, D), router_w, router_b, exp_w1[l], exp_w3[l], exp_w2[l])
            xl = xl + mod_l[5] * y.reshape(hl.shape)
    return rms_norm(xl, final_g)
```

```python
import functools
import math

import jax
import jax.numpy as jnp
from jax import lax
from jax.experimental import pallas as pl
from jax.experimental.pallas import tpu as pltpu

F32 = jnp.float32
BF16 = jnp.bfloat16
I32 = jnp.int32
MXU_DTYPE = BF16

EPS = 1e-6
GRID_W = 64
N_MOD = 6

GLA_HEADS = 4
GLA_DK = 128
GLA_DV = 256
GLA_RANK = 16
GLA_TAU = 16.0
GLA_CHUNK = 64

FNET_GROUPS = 4

SSD_HEADDIM = 64
SSD_GROUPS = 4
SSD_STATE = 128
SSD_CONV = 5
SSD_CHUNK = 128
HEADS_PER_GROUP = 8

N_EXPERTS = 16
N_EXPERT_GROUPS = 4
EXPERTS_PER_GROUP = 4
D_FF = 512
N_PAIRS = 6
N_BUCKETS = N_EXPERT_GROUPS * N_PAIRS
PAIR_LO = (0, 0, 0, 1, 1, 2)
PAIR_HI = (1, 2, 3, 2, 3, 3)

TM = 256
TMX = 256
LANES = 128
SUB = 8
VMEM_LIMIT = 56 * 1024 * 1024

COL_Q, COL_K, COL_V, COL_R, COL_U, COL_Z, COL_X, COL_B, COL_C, COL_GA, COL_GB, COL_GC = (
    0, 512, 1024, 2048, 3072, 4096, 6144, 8192, 8704, 9216, 10240, 11264)
P_MAIN = 12288
P_SMALL = LANES * (1 + SSD_GROUPS)
IN_TN = 512


def _cparams(sem, vmem=VMEM_LIMIT):
    return pltpu.CompilerParams(dimension_semantics=sem, vmem_limit_bytes=vmem)


def _mm(a, b):
    return jnp.dot(a.astype(MXU_DTYPE), b.astype(MXU_DTYPE), preferred_element_type=F32)


def _mm_nt(a, b):
    return lax.dot_general(a.astype(MXU_DTYPE), b.astype(MXU_DTYPE), (((1,), (1,)), ((), ())),
                           preferred_element_type=F32)


def _mm_tn(a, b):
    return lax.dot_general(a.astype(MXU_DTYPE), b.astype(MXU_DTYPE), (((0,), (0,)), ((), ())),
                           preferred_element_type=F32)


def _split3(x):
    hi = x.astype(MXU_DTYPE)
    r1 = x - hi.astype(F32)
    mid = r1.astype(MXU_DTYPE)
    lo = (r1 - mid.astype(F32)).astype(MXU_DTYPE)
    return hi, mid, lo


def _mm_exact_lhs(m01, x):
    hi, mid, lo = _split3(x)
    m = m01.astype(MXU_DTYPE)
    return (jnp.dot(m, hi, preferred_element_type=F32) + jnp.dot(m, mid, preferred_element_type=F32)
            + jnp.dot(m, lo, preferred_element_type=F32))


def _sigmoid(x):
    return 1.0 / (1.0 + jnp.exp(-x))


def _silu(x):
    return x * _sigmoid(x)


def _softplus(x):
    return jnp.maximum(x, 0.0) + jnp.log(1.0 + jnp.exp(-jnp.abs(x)))


def _rms(x, g):
    return x * lax.rsqrt(jnp.mean(x * x, axis=-1, keepdims=True) + EPS) * g


def _modulate(x, g, shift, scale):
    return _rms(x, g) * (1.0 + scale) + shift


def _ada_kernel(c_ref, w_ref, b_ref, o_ref):
    o_ref[0] = _mm(_silu(c_ref[...]), w_ref[0]) + b_ref[0]


def _ada(c_all, ada_w, ada_b):
    depth, d, nd = ada_w.shape
    rows = c_all.shape[0]
    tn = 1024
    return pl.pallas_call(
        _ada_kernel,
        grid=(depth, nd // tn),
        in_specs=[pl.BlockSpec((rows, d), lambda l, j: (0, 0)),
                  pl.BlockSpec((1, d, tn), lambda l, j: (l, 0, j)),
                  pl.BlockSpec((1, 1, tn), lambda l, j: (l, 0, j))],
        out_specs=pl.BlockSpec((1, rows, tn), lambda l, j: (l, 0, j)),
        out_shape=jax.ShapeDtypeStruct((depth, rows, nd), F32),
        compiler_params=_cparams(("arbitrary", "arbitrary")),
        name="ada_mod",
    )(c_all, ada_w, ada_b.reshape(depth, 1, nd))


def _normmod_kernel(x_ref, g_ref, mod_ref, h_ref):
    mod = mod_ref[0, 0]
    h_ref[0] = _modulate(x_ref[0], g_ref[...], mod[0:1], mod[1:2]).astype(h_ref.dtype)


def _normmod(x, g, modsel, n_ctx_tiles):
    b, s, d = x.shape
    return pl.pallas_call(
        _normmod_kernel,
        grid=(b, s // TM),
        in_specs=[pl.BlockSpec((1, TM, d), lambda i, j: (i, j, 0)),
                  pl.BlockSpec((1, d), lambda i, j: (0, 0)),
                  pl.BlockSpec((1, 1, SUB, d), lambda i, j: (i, jnp.where(j < n_ctx_tiles, 0, 1), 0, 0))],
        out_specs=pl.BlockSpec((1, TM, d), lambda i, j: (i, j, 0)),
        out_shape=jax.ShapeDtypeStruct((b, s, d), MXU_DTYPE),
        compiler_params=_cparams(("arbitrary", "arbitrary")),
        name="norm_mod",
    )(x, g.reshape(1, d), modsel)


def _inproj_kernel(h_ref, w_ref, ws_ref, p_ref, ps_ref):
    h = h_ref[0]
    p_ref[0] = _mm(h, w_ref[...]).astype(p_ref.dtype)

    @pl.when(pl.program_id(1) == 0)
    def _():
        ps_ref[0] = _mm(h, ws_ref[...])


def _inproj(h, w_main, w_small):
    b, s, d = h.shape
    return pl.pallas_call(
        _inproj_kernel,
        grid=(b, P_MAIN // IN_TN),
        in_specs=[pl.BlockSpec((1, s, d), lambda i, j: (i, 0, 0)),
                  pl.BlockSpec((d, IN_TN), lambda i, j: (0, j)),
                  pl.BlockSpec((d, P_SMALL), lambda i, j: (0, 0))],
        out_specs=[pl.BlockSpec((1, s, IN_TN), lambda i, j: (i, 0, j)),
                   pl.BlockSpec((1, s, P_SMALL), lambda i, j: (i, 0, 0))],
        out_shape=[jax.ShapeDtypeStruct((b, s, P_MAIN), MXU_DTYPE),
                   jax.ShapeDtypeStruct((b, s, P_SMALL), F32)],
        compiler_params=_cparams(("arbitrary", "arbitrary")),
        name="in_proj",
    )(h, w_main, w_small)


def _conv_kernel(p_ref, w_ref, b_ref, o_ref, *, n_ctx):
    x = p_ref[0].astype(F32)
    s = x.shape[0]
    w = w_ref[...]
    row = lax.broadcasted_iota(I32, (s, 1), 0)
    seg_lo = jnp.where(row < n_ctx, 0, n_ctx)
    seg_hi = jnp.where(row < n_ctx, n_ctx, s)
    acc = jnp.zeros_like(x) + b_ref[...]
    half = SSD_CONV // 2
    for k in range(SSD_CONV):
        dlt = k - half
        xs = x if dlt == 0 else pltpu.roll(x, shift=(-dlt) % s, axis=0)
        t2 = row + dlt
        ok = jnp.broadcast_to(jnp.where((t2 >= seg_lo) & (t2 < seg_hi), 1.0, 0.0), x.shape)
        acc = acc + w[k:k + 1] * jnp.where(ok > 0.5, xs, 0.0)
    o_ref[0] = _silu(acc).astype(o_ref.dtype)


def _conv(p, conv_w, conv_b, n_ctx):
    b, s, _ = p.shape
    c = conv_w.shape[1]
    tc = 256
    return pl.pallas_call(
        functools.partial(_conv_kernel, n_ctx=n_ctx),
        grid=(b, c // tc),
        in_specs=[pl.BlockSpec((1, s, tc), lambda i, j: (i, 0, COL_X // tc + j)),
                  pl.BlockSpec((SSD_CONV, tc), lambda i, j: (0, j)),
                  pl.BlockSpec((1, tc), lambda i, j: (0, j))],
        out_specs=pl.BlockSpec((1, s, tc), lambda i, j: (i, 0, j)),
        out_shape=jax.ShapeDtypeStruct((b, s, c), MXU_DTYPE),
        compiler_params=_cparams(("arbitrary", "arbitrary")),
        name="ssd_conv",
    )(p, conv_w, conv_b.reshape(1, c))


def _gla_kernel(q_ref, k_ref, v_ref, r_ref, ps_ref, aw_ref, ab_ref, g_ref, o_ref,
                laf_s, lab_s, oacc_s, *, n_ctx):
    L = GLA_CHUNK
    s = q_ref.shape[1]
    n = s // L
    nc = n_ctx // L
    lr = ps_ref[0]
    for d, la_s in ((0, laf_s), (1, lab_s)):
        pre = _mm(lr, aw_ref[d]) + ab_ref[d]
        la_s[...] = -_softplus(-pre) * (1.0 / GLA_TAU)

    ri = lax.broadcasted_iota(I32, (L, L), 0)
    ci = lax.broadcasted_iota(I32, (L, L), 1)
    scale = GLA_DK ** -0.5

    def chunk(off, st, reverse):
        la_s = lab_s if reverse else laf_s
        q = q_ref[0, pl.ds(off, L), :].astype(F32) * scale
        k = k_ref[0, pl.ds(off, L), :].astype(F32)
        v = v_ref[0, pl.ds(off, L), :]
        la = la_s[pl.ds(off, L), :]
        keep = (ci >= ri) if reverse else (ci <= ri)
        bcum = _mm_exact_lhs(jnp.where(keep, 1.0, 0.0), la)
        i_ref = L // 2 if reverse else L // 2 - 1
        i_last = 0 if reverse else L - 1
        b_mid = bcum[i_ref:i_ref + 1]
        b_last = bcum[i_last:i_last + 1]
        sc = _mm_nt(q * jnp.exp(bcum - b_mid), k * jnp.exp(b_mid - bcum))
        sc = jnp.where(keep, sc, 0.0)
        o = _mm(sc, v) + _mm_nt(q * jnp.exp(bcum), st)
        st = jnp.exp(b_last) * st + _mm_tn(v, k * jnp.exp(b_last - bcum))
        return o, st

    def fwd(c, st):
        off = pl.multiple_of(c * L, L)
        o, st = chunk(off, st, False)
        oacc_s[pl.ds(off, L), :] = o
        return st

    lax.fori_loop(0, n, fwd, jnp.zeros((GLA_DV, GLA_DK), F32))

    g = g_ref[...]

    def bwd(i, st):
        c = jnp.where(i < nc, nc - 1 - i, n + nc - 1 - i)
        off = pl.multiple_of(c * L, L)
        o, st = chunk(off, st, True)
        o = o + oacc_s[pl.ds(off, L), :]
        r = r_ref[0, pl.ds(off, L), :].astype(F32)
        o_ref[0, pl.ds(off, L), :] = (_rms(o, g) * _silu(r)).astype(o_ref.dtype)
        return st

    lax.fori_loop(0, n, bwd, jnp.zeros((GLA_DV, GLA_DK), F32))


def _gla(p, ps, aw_pad, ab, norm_g, n_ctx):
    b, s, _ = p.shape
    dk, dv = GLA_DK, GLA_DV
    return pl.pallas_call(
        functools.partial(_gla_kernel, n_ctx=n_ctx),
        grid=(b, GLA_HEADS),
        in_specs=[pl.BlockSpec((1, s, dk), lambda i, h: (i, 0, COL_Q // dk + h)),
                  pl.BlockSpec((1, s, dk), lambda i, h: (i, 0, COL_K // dk + h)),
                  pl.BlockSpec((1, s, dv), lambda i, h: (i, 0, COL_V // dv + h)),
                  pl.BlockSpec((1, s, dv), lambda i, h: (i, 0, COL_R // dv + h)),
                  pl.BlockSpec((1, s, LANES), lambda i, h: (i, 0, 0)),
                  pl.BlockSpec((2, LANES, dk), lambda i, h: (0, 0, h)),
                  pl.BlockSpec((2, 1, dk), lambda i, h: (0, 0, h)),
                  pl.BlockSpec((1, dv), lambda i, h: (0, 0))],
        out_specs=pl.BlockSpec((1, s, dv), lambda i, h: (i, 0, h)),
        out_shape=jax.ShapeDtypeStruct((b, s, GLA_HEADS * dv), MXU_DTYPE),
        scratch_shapes=[pltpu.VMEM((s, dk), F32), pltpu.VMEM((s, dk), F32), pltpu.VMEM((s, dv), F32)],
        compiler_params=_cparams(("arbitrary", "arbitrary")),
        name="gla",
    )(p, p, p, p, ps, aw_pad, ab, norm_g.reshape(1, dv))


def _ssd_kernel(z_ref, x_ref, bm_ref, cm_ref, ps_ref, brow_ref, alog_ref, d_ref, o_ref,
                dt_s, w_s, yacc_s, st_s, *, n_ctx):
    L = SSD_CHUNK
    s = x_ref.shape[1]
    n = s // L
    nc = n_ctx // L
    hp = HEADS_PER_GROUP
    dtall = _softplus(ps_ref[0] + brow_ref[0])
    dt_s[...] = dtall
    w_s[...] = dtall * (-jnp.exp(alog_ref[0]))

    ri = lax.broadcasted_iota(I32, (L, L), 0)
    ci = lax.broadcasted_iota(I32, (L, L), 1)
    lane = lax.broadcasted_iota(I32, (1, LANES), 1)
    low = lane < SSD_HEADDIM
    npair = hp // 2

    def expand(cols, base):
        blocks = []
        for pr in range(npair):
            a = cols[:, base + 2 * pr:base + 2 * pr + 1]
            b = cols[:, base + 2 * pr + 1:base + 2 * pr + 2]
            blocks.append(jnp.where(low, a, b))
        return jnp.concatenate(blocks, axis=1)

    def chunk(off, reverse):
        base = hp if reverse else 0
        keep = (ci >= ri) if reverse else (ci <= ri)
        x = x_ref[0, pl.ds(off, L), :].astype(F32)
        bm = bm_ref[0, pl.ds(off, L), :]
        cm = cm_ref[0, pl.ds(off, L), :]
        dtc = dt_s[pl.ds(off, L), :]
        wc = w_s[pl.ds(off, L), :]
        acs = _mm_exact_lhs(jnp.where(keep, 1.0, 0.0), wc)
        acs_t = acs.T
        i_last = 0 if reverse else L - 1
        xdt = x * expand(dtc, base)
        acs_e = expand(acs, base)
        acs_last = acs_e[i_last:i_last + 1]
        cb = _mm_nt(cm, bm)
        st = st_s[...]
        y_off = _mm(cm, st) * jnp.exp(acs_e)
        ys = []
        for pr in range(npair):
            ms = []
            for j in (2 * pr, 2 * pr + 1):
                sg = acs[:, base + j:base + j + 1] - acs_t[base + j:base + j + 1, :]
                ms.append(cb * jnp.exp(jnp.where(keep, sg, -1e30)))
            m2 = jnp.concatenate(ms, axis=1)
            xp = xdt[:, pr * LANES:(pr + 1) * LANES]
            bd = jnp.concatenate([jnp.where(low, xp, 0.0), jnp.where(low, 0.0, xp)], axis=0)
            ys.append(_mm(m2, bd))
        y = jnp.concatenate(ys, axis=1) + y_off
        st_s[...] = jnp.exp(acs_last) * st + _mm_tn(bm, xdt * jnp.exp(acs_last - acs_e))
        return y, x

    st_s[...] = jnp.zeros_like(st_s)

    @pl.loop(0, n)
    def _(c):
        off = pl.multiple_of(c * L, L)
        y, _x = chunk(off, False)
        yacc_s[pl.ds(off, L), :] = y

    st_s[...] = jnp.zeros_like(st_s)
    dsk = d_ref[0]

    @pl.loop(0, n)
    def _(i):
        c = jnp.where(i < nc, nc - 1 - i, n + nc - 1 - i)
        off = pl.multiple_of(c * L, L)
        y, x = chunk(off, True)
        y = y + yacc_s[pl.ds(off, L), :] + dsk * x
        z = z_ref[0, pl.ds(off, L), :].astype(F32)
        o_ref[0, pl.ds(off, L), :] = (y * _silu(z)).astype(o_ref.dtype)


def _ssd(p, xbc, ps, brow, alog_row, d_exp, n_ctx):
    b, s, _ = p.shape
    gw = HEADS_PER_GROUP * SSD_HEADDIM
    inner = SSD_GROUPS * gw
    ns = SSD_STATE
    return pl.pallas_call(
        functools.partial(_ssd_kernel, n_ctx=n_ctx),
        grid=(b, SSD_GROUPS),
        in_specs=[pl.BlockSpec((1, s, gw), lambda i, g: (i, 0, COL_Z // gw + g)),
                  pl.BlockSpec((1, s, gw), lambda i, g: (i, 0, g)),
                  pl.BlockSpec((1, s, ns), lambda i, g: (i, 0, inner // ns + g)),
                  pl.BlockSpec((1, s, ns), lambda i, g: (i, 0, inner // ns + SSD_GROUPS + g)),
                  pl.BlockSpec((1, s, LANES), lambda i, g: (i, 0, 1 + g)),
                  pl.BlockSpec((1, 1, LANES), lambda i, g: (g, 0, 0)),
                  pl.BlockSpec((1, 1, LANES), lambda i, g: (g, 0, 0)),
                  pl.BlockSpec((1, 1, gw), lambda i, g: (g, 0, 0))],
        out_specs=pl.BlockSpec((1, s, gw), lambda i, g: (i, 0, g)),
        out_shape=jax.ShapeDtypeStruct((b, s, inner), MXU_DTYPE),
        scratch_shapes=[pltpu.VMEM((s, LANES), F32), pltpu.VMEM((s, LANES), F32),
                        pltpu.VMEM((s, gw), F32), pltpu.VMEM((ns, gw), F32)],
        compiler_params=_cparams(("arbitrary", "arbitrary")),
        name="ssd",
    )(p, xbc, xbc, xbc, ps, brow, alog_row, d_exp)


def _chan_dft_kernel(u_ref, w_ref, o_ref):
    u = u_ref[0]
    gd = w_ref.shape[0]
    w = w_ref[...]
    cs, sn = [], []
    for g in range(FNET_GROUPS):
        r = _mm(u[:, g * gd:(g + 1) * gd], w)
        cs.append(r[:, :gd])
        sn.append(r[:, gd:])
    o_ref[0] = jnp.concatenate(cs + sn, axis=1).astype(o_ref.dtype)


def _chan_dft(p, w_cs, width):
    b, s, _ = p.shape
    gd = width // FNET_GROUPS
    return pl.pallas_call(
        _chan_dft_kernel,
        grid=(b, s // TM),
        in_specs=[pl.BlockSpec((1, TM, width), lambda i, j: (i, j, COL_U // width)),
                  pl.BlockSpec((gd, 2 * gd), lambda i, j: (0, 0))],
        out_specs=pl.BlockSpec((1, TM, 2 * width), lambda i, j: (i, j, 0)),
        out_shape=jax.ShapeDtypeStruct((b, s, 2 * width), MXU_DTYPE),
        compiler_params=_cparams(("arbitrary", "arbitrary")),
        name="fnet_chan",
    )(p, w_cs)


def _pos_dft_kernel(cc_ref, sc_ref, cl_ref, sl_ref, uc_ref, us_ref, o_ref, *, n_ctx):
    j = pl.program_id(1)
    nct = n_ctx // TM

    @pl.when(j < nct)
    def _():
        o_ref[0] = (_mm(cc_ref[...], uc_ref[0, :n_ctx, :]) - _mm(sc_ref[...], us_ref[0, :n_ctx, :])).astype(o_ref.dtype)

    @pl.when(j >= nct)
    def _():
        o_ref[0] = (_mm(cl_ref[...], uc_ref[0, n_ctx:, :]) - _mm(sl_ref[...], us_ref[0, n_ctx:, :])).astype(o_ref.dtype)


def _pos_dft(v, cos_c, sin_c, cos_l, sin_l, width):
    b, s, _ = v.shape
    n_ctx = cos_c.shape[0]
    t = cos_l.shape[0]
    nct = n_ctx // TM
    cidx = lambda i, j: (jnp.minimum(j, nct - 1), 0)
    lidx = lambda i, j: (jnp.maximum(j - nct, 0), 0)
    return pl.pallas_call(
        functools.partial(_pos_dft_kernel, n_ctx=n_ctx),
        grid=(b, s // TM),
        in_specs=[pl.BlockSpec((TM, n_ctx), cidx),
                  pl.BlockSpec((TM, n_ctx), cidx),
                  pl.BlockSpec((TM, t), lidx),
                  pl.BlockSpec((TM, t), lidx),
                  pl.BlockSpec((1, s, width), lambda i, j: (i, 0, 0)),
                  pl.BlockSpec((1, s, width), lambda i, j: (i, 0, 1))],
        out_specs=pl.BlockSpec((1, TM, width), lambda i, j: (i, j, 0)),
        out_shape=jax.ShapeDtypeStruct((b, s, width), MXU_DTYPE),
        compiler_params=_cparams(("arbitrary", "arbitrary")),
        name="fnet_pos",
    )(cos_c, sin_c, cos_l, sin_l, v, v)


def _merge_kernel(ga_ref, fm_ref, yz_ref, g1_ref, g2_ref, g3_ref, x_ref, mod_ref, n2g_ref, sg_ref,
                  wa_ref, wb_ref, wc_ref, wo_ref, rw_ref, rb_ref,
                  xo_ref, ht_ref, ri_ref, rf_ref, wcol_ref, cnt_ref, carry_s):
    first = (pl.program_id(0) == 0) & (pl.program_id(1) == 0)

    @pl.when(first)
    def _():
        carry_s[...] = jnp.zeros_like(carry_s)

    mod = mod_ref[0, 0]
    ya = _mm(ga_ref[0], wa_ref[...])
    yb = _mm(fm_ref[0], wb_ref[...])
    yz = yz_ref[0].astype(F32)
    yc = _mm(_rms(yz, sg_ref[...]), wc_ref[...])
    y = (_sigmoid(g1_ref[0].astype(F32)) * ya + _sigmoid(g2_ref[0].astype(F32)) * yb
         + _sigmoid(g3_ref[0].astype(F32)) * yc)
    xn = x_ref[0] + mod[2:3] * _mm(y, wo_ref[...])
    xo_ref[0] = xn
    h2 = _modulate(xn, n2g_ref[...], mod[3:4], mod[4:5])
    d = h2.shape[1]
    tm = h2.shape[0]
    for sidx in range(d // LANES):
        ht_ref[pl.ds(sidx, tm, stride=d // LANES), :] = h2[:, sidx * LANES:(sidx + 1) * LANES]

    hh, hm, _hl = _split3(h2)
    rh, rm, _rl = _split3(rw_ref[...])
    nt = (((1,), (1,)), ((), ()))
    lg = (lax.dot_general(rh, hh, nt, preferred_element_type=F32)
          + lax.dot_general(rh, hm, nt, preferred_element_type=F32)
          + lax.dot_general(rm, hh, nt, preferred_element_type=F32))
    sc = _sigmoid(lg)
    sel = sc + rb_ref[...]
    srow = [sel[e:e + 1] for e in range(N_EXPERTS)]
    urow = [sc[e:e + 1] for e in range(N_EXPERTS)]
    epg = EXPERTS_PER_GROUP

    best_v = best_pair = best_g = best_slo = best_shi = None
    for g in range(N_EXPERT_GROUPS):
        a = srow[g * epg:(g + 1) * epg]
        u = urow[g * epg:(g + 1) * epg]
        gs = None
        for i in range(epg):
            for j in range(i + 1, epg):
                ps_ = a[i] + a[j]
                gs = ps_ if gs is None else jnp.maximum(gs, ps_)
        m1, i1 = a[0], jnp.zeros_like(a[0], dtype=I32)
        for i in range(1, epg):
            up = a[i] > m1
            i1 = jnp.where(up, i, i1)
            m1 = jnp.where(up, a[i], m1)
        m2 = jnp.full_like(a[0], -jnp.inf)
        i2 = jnp.full_like(i1, -1)
        for i in range(epg):
            cand = (i1 != i) & (a[i] > m2)
            i2 = jnp.where(cand, i, i2)
            m2 = jnp.where(cand, a[i], m2)
        lo = jnp.minimum(i1, i2)
        hi = jnp.maximum(i1, i2)
        pair = jnp.where(lo == 0, 0, jnp.where(lo == 1, 3, 5)) + (hi - lo - 1)
        s_lo = jnp.where(lo == 0, u[0], jnp.where(lo == 1, u[1], u[2]))
        s_hi = jnp.where(hi == 1, u[1], jnp.where(hi == 2, u[2], u[3]))
        if g == 0:
            best_v, best_pair, best_g, best_slo, best_shi = gs, pair, jnp.zeros_like(pair), s_lo, s_hi
        else:
            up = gs > best_v
            best_v = jnp.where(up, gs, best_v)
            best_pair = jnp.where(up, pair, best_pair)
            best_g = jnp.where(up, g, best_g)
            best_slo = jnp.where(up, s_lo, best_slo)
            best_shi = jnp.where(up, s_hi, best_shi)
    bucket = best_g * N_PAIRS + best_pair
    tot = best_slo + best_shi
    w_lo = best_slo / tot
    w_hi = best_shi / tot

    nb = carry_s.shape[0]
    oh = (lax.broadcasted_iota(I32, (nb, tm), 0) == bucket).astype(F32)
    tri = (lax.broadcasted_iota(I32, (tm, tm), 0) <= lax.broadcasted_iota(I32, (tm, tm), 1))
    csum = _mm(oh, jnp.where(tri, 1.0, 0.0))
    carry = carry_s[...]
    rank = jnp.sum(oh * (csum - 1.0 + carry), axis=0, keepdims=True)
    carry = carry + jnp.sum(oh, axis=1, keepdims=True)
    carry_s[...] = carry
    cnt_ref[...] = jnp.broadcast_to(carry, cnt_ref.shape).astype(I32)

    zi = jnp.zeros((SUB - 2, tm), I32)
    ri_ref[...] = jnp.concatenate([bucket, rank.astype(I32), zi], axis=0)
    wrows = jnp.concatenate([w_lo, w_hi, jnp.zeros((SUB - 2, tm), F32)], axis=0)
    rf_ref[...] = wrows
    eye = (lax.broadcasted_iota(I32, (SUB, LANES), 0) == lax.broadcasted_iota(I32, (SUB, LANES), 1))
    eye = jnp.where(eye, 1.0, 0.0).astype(MXU_DTYPE)
    tn = (((0,), (0,)), ((), ()))
    p1, p2, p3 = _split3(wrows)
    wcol_ref[...] = (lax.dot_general(p1, eye, tn, preferred_element_type=F32)
                     + lax.dot_general(p2, eye, tn, preferred_element_type=F32)
                     + lax.dot_general(p3, eye, tn, preferred_element_type=F32))


def _merge(ga, fm, yz, p, x, modsel, n2g, ssd_g, w_a, w_b, w_c, w_o, rw_t, rb, n_ctx_tiles):
    b, s, d = x.shape
    nt = s // TM
    n = b * s
    di = yz.shape[2]
    nbp = 32
    tok = lambda i, j: (i, j, 0)
    flat = lambda i, j: (0, i * nt + j)
    const = lambda i, j: (0, 0)
    return pl.pallas_call(
        _merge_kernel,
        grid=(b, nt),
        in_specs=[pl.BlockSpec((1, TM, d), tok),
                  pl.BlockSpec((1, TM, d), tok),
                  pl.BlockSpec((1, TM, di), tok),
                  pl.BlockSpec((1, TM, d), lambda i, j: (i, j, COL_GA // d)),
                  pl.BlockSpec((1, TM, d), lambda i, j: (i, j, COL_GB // d)),
                  pl.BlockSpec((1, TM, d), lambda i, j: (i, j, COL_GC // d)),
                  pl.BlockSpec((1, TM, d), tok),
                  pl.BlockSpec((1, 1, SUB, d), lambda i, j: (i, jnp.where(j < n_ctx_tiles, 0, 1), 0, 0)),
                  pl.BlockSpec((1, d), const),
                  pl.BlockSpec((1, di), const),
                  pl.BlockSpec((d, d), const),
                  pl.BlockSpec((d, d), const),
                  pl.BlockSpec((di, d), const),
                  pl.BlockSpec((d, d), const),
                  pl.BlockSpec((N_EXPERTS, d), const),
                  pl.BlockSpec((N_EXPERTS, 1), const)],
        out_specs=[pl.BlockSpec((1, TM, d), tok),
                   pl.BlockSpec((TM * (d // LANES), LANES), lambda i, j: (i * nt + j, 0)),
                   pl.BlockSpec((SUB, TM), flat),
                   pl.BlockSpec((SUB, TM), flat),
                   pl.BlockSpec((TM, LANES), lambda i, j: (i * nt + j, 0)),
                   pl.BlockSpec((nbp, LANES), const)],
        out_shape=[jax.ShapeDtypeStruct((b, s, d), F32),
                   jax.ShapeDtypeStruct((n * (d // LANES), LANES), F32),
                   jax.ShapeDtypeStruct((SUB, n), I32),
                   jax.ShapeDtypeStruct((SUB, n), F32),
                   jax.ShapeDtypeStruct((n, LANES), F32),
                   jax.ShapeDtypeStruct((nbp, LANES), I32)],
        scratch_shapes=[pltpu.VMEM((nbp, 1), F32)],
        compiler_params=_cparams(("arbitrary", "arbitrary")),
        name="merge_route",
    )(ga, fm, yz, p, p, p, x, modsel, n2g, ssd_g, w_a, w_b, w_c, w_o, rw_t, rb)


def _row_copy(src, dst, src_row, dst_row, rows, sem):
    return pltpu.make_async_copy(src.at[pl.ds(pl.multiple_of(src_row * rows, rows), rows)],
                                 dst.at[pl.ds(pl.multiple_of(dst_row * rows, rows), rows)], sem)


def _dispatch_kernel(pos_ref, ht_ref, xs_in_ref, xs_ref, sem, *, rows):
    del xs_in_ref
    base = pl.program_id(0) * TM

    @pl.loop(0, TM)
    def _(t):
        _row_copy(ht_ref, xs_ref, base + t, pos_ref[0, t], rows, sem).start()

    @pl.loop(0, TM)
    def _(t):
        _row_copy(ht_ref, xs_ref, base + t, pos_ref[0, t], rows, sem).wait()


def _dispatch(pos, ht, xs_init, rows):
    n = pos.shape[1]
    return pl.pallas_call(
        functools.partial(_dispatch_kernel, rows=rows),
        grid=(n // TM,),
        in_specs=[pl.BlockSpec((1, TM), lambda i: (0, i), memory_space=pltpu.SMEM),
                  pl.BlockSpec(memory_space=pl.ANY),
                  pl.BlockSpec(memory_space=pl.ANY)],
        out_specs=pl.BlockSpec(memory_space=pl.ANY),
        out_shape=jax.ShapeDtypeStruct(xs_init.shape, xs_init.dtype),
        input_output_aliases={2: 0},
        scratch_shapes=[pltpu.SemaphoreType.DMA(())],
        compiler_params=_cparams(("arbitrary",)),
        name="moe_dispatch",
    )(pos, ht, xs_init)


def _expert_kernel(ta_ref, tb_ref, nv_ref, xs_ref, w13a_ref, w2a_ref, w13b_ref, w2b_ref, ys_ref):
    del ta_ref, tb_ref

    @pl.when(pl.program_id(0) >= nv_ref[0])
    def _():
        ys_ref[...] = jnp.zeros_like(ys_ref)

    @pl.when(pl.program_id(0) < nv_ref[0])
    def _():
        ns = xs_ref.shape[0] // TMX
        x = jnp.concatenate([xs_ref[pl.ds(sidx, TMX, stride=ns), :] for sidx in range(ns)], axis=1)
        x = x.astype(MXU_DTYPE)
        for kk, (w13_ref, w2_ref) in enumerate(((w13a_ref, w2a_ref), (w13b_ref, w2b_ref))):
            h13 = _mm(x, w13_ref[0])
            he = _silu(h13[:, :D_FF]) * h13[:, D_FF:]
            y = _mm(he, w2_ref[0])
            for sidx in range(ns):
                ys_ref[pl.ds(kk * ns + sidx, TMX, stride=2 * ns), :] = y[:, sidx * LANES:(sidx + 1) * LANES]


def _experts(tile_a, tile_b, nvalid, xs, w13, w2, d):
    ns = d // LANES
    ntile = xs.shape[0] // (TMX * ns)
    last = lambda i, nv: jnp.minimum(i, nv[0] - 1)
    grid_spec = pltpu.PrefetchScalarGridSpec(
        num_scalar_prefetch=3,
        grid=(ntile,),
        in_specs=[pl.BlockSpec((TMX * ns, LANES), lambda i, ta, tb, nv: (last(i, nv), 0)),
                  pl.BlockSpec((1, d, 2 * D_FF), lambda i, ta, tb, nv: (ta[last(i, nv)], 0, 0)),
                  pl.BlockSpec((1, D_FF, d), lambda i, ta, tb, nv: (ta[last(i, nv)], 0, 0)),
                  pl.BlockSpec((1, d, 2 * D_FF), lambda i, ta, tb, nv: (tb[last(i, nv)], 0, 0)),
                  pl.BlockSpec((1, D_FF, d), lambda i, ta, tb, nv: (tb[last(i, nv)], 0, 0))],
        out_specs=pl.BlockSpec((TMX * 2 * ns, LANES), lambda i, ta, tb, nv: (i, 0)),
    )
    return pl.pallas_call(
        _expert_kernel,
        grid_spec=grid_spec,
        out_shape=jax.ShapeDtypeStruct((ntile * TMX * 2 * ns, LANES), F32),
        compiler_params=_cparams(("arbitrary",)),
        name="moe_experts",
    )(tile_a, tile_b, nvalid, xs, w13, w2, w13, w2)


def _combine_kernel(pos_ref, ys_ref, x_ref, wcol_ref, mod_ref, modn_ref, g_ref, xo_ref, ho_ref, buf, sem,
                    *, rows, final):
    @pl.loop(0, TM)
    def _(t):
        _row_copy(ys_ref, buf, pos_ref[0, t], t, rows, sem).start()

    @pl.loop(0, TM)
    def _(t):
        _row_copy(ys_ref, buf, pos_ref[0, t], t, rows, sem).wait()

    ns = rows // 2
    ya = jnp.concatenate([buf[pl.ds(sidx, TM, stride=rows), :] for sidx in range(ns)], axis=1)
    yb = jnp.concatenate([buf[pl.ds(ns + sidx, TM, stride=rows), :] for sidx in range(ns)], axis=1)
    wc = wcol_ref[...]
    y = wc[:, 0:1] * ya + wc[:, 1:2] * yb
    mod = mod_ref[0, 0]
    xn = x_ref[0] + mod[5:6] * y
    xo_ref[0] = xn
    if final:
        ho_ref[0] = _rms(xn, g_ref[...]).astype(ho_ref.dtype)
    else:
        modn = modn_ref[0, 0]
        ho_ref[0] = _modulate(xn, g_ref[...], modn[0:1], modn[1:2]).astype(ho_ref.dtype)


def _combine(pos, ys, x, wcol, modsel, modsel_next, g_next, n_ctx_tiles, final):
    b, s, d = x.shape
    nt = s // TM
    rows = 2 * (d // LANES)
    tok = lambda i, j: (i, j, 0)
    msel = lambda i, j: (i, jnp.where(j < n_ctx_tiles, 0, 1), 0, 0)
    if final:
        t_lat = s - n_ctx_tiles * TM
        h_shape = jax.ShapeDtypeStruct((b, t_lat, d), F32)
        h_spec = pl.BlockSpec((1, TM, d), lambda i, j: (i, jnp.maximum(j - n_ctx_tiles, 0), 0))
    else:
        h_shape = jax.ShapeDtypeStruct((b, s, d), MXU_DTYPE)
        h_spec = pl.BlockSpec((1, TM, d), tok)
    return pl.pallas_call(
        functools.partial(_combine_kernel, rows=rows, final=final),
        grid=(b, nt),
        in_specs=[pl.BlockSpec((1, TM), lambda i, j: (0, i * nt + j), memory_space=pltpu.SMEM),
                  pl.BlockSpec(memory_space=pl.ANY),
                  pl.BlockSpec((1, TM, d), tok),
                  pl.BlockSpec((TM, LANES), lambda i, j: (i * nt + j, 0)),
                  pl.BlockSpec((1, 1, SUB, d), msel),
                  pl.BlockSpec((1, 1, SUB, d), msel),
                  pl.BlockSpec((1, d), lambda i, j: (0, 0))],
        out_specs=[pl.BlockSpec((1, TM, d), tok), h_spec],
        out_shape=[jax.ShapeDtypeStruct((b, s, d), F32), h_shape],
        scratch_shapes=[pltpu.VMEM((TM * rows, LANES), F32), pltpu.SemaphoreType.DMA(())],
        compiler_params=_cparams(("arbitrary", "arbitrary")),
        name="moe_combine_final" if final else "moe_combine",
    )(pos, ys, x, wcol, modsel, modsel_next, g_next.reshape(1, d))


def _grid_sincos(n_tokens, dim):
    rows = n_tokens // GRID_W
    row = jnp.broadcast_to(jnp.arange(rows)[:, None], (rows, GRID_W)).reshape(-1).astype(F32)
    col = jnp.broadcast_to(jnp.arange(GRID_W)[None, :], (rows, GRID_W)).reshape(-1).astype(F32)
    quarter = dim // 4
    freqs = jnp.exp(-math.log(10000.0) * jnp.arange(quarter, dtype=F32) / quarter)
    ar = row[:, None] * freqs
    ac = col[:, None] * freqs
    return jnp.concatenate([jnp.sin(ar), jnp.cos(ar), jnp.sin(ac), jnp.cos(ac)], axis=-1)


def _dft_mats(n, scale):
    idx = jnp.arange(n, dtype=I32)
    ang = ((idx[:, None] * idx[None, :]) % n).astype(F32) * (2.0 * math.pi / n)
    return jnp.cos(ang) * scale, jnp.sin(ang) * scale


def _split_cols(w, sizes):
    out, start = [], 0
    for sz in sizes:
        out.append(w[..., start:start + sz])
        start += sz
    return out


def _layer_weights(d, w_in_l, gla_a_w_l, gla_a_b_l, dt_bias_l, a_log_l, ssd_d_l):
    qk = GLA_HEADS * GLA_DK
    vv = GLA_HEADS * GLA_DV
    inner = SSD_GROUPS * HEADS_PER_GROUP * SSD_HEADDIM
    gn = SSD_GROUPS * SSD_STATE
    nh = SSD_GROUPS * HEADS_PER_GROUP
    sizes = (qk, qk, vv, vv, GLA_RANK, GLA_RANK, d, inner, inner + 2 * gn, nh, nh, d, d, d)
    (wq, wk, wv, wr, wlf, wlb, wu, wz, wxbc, wdf, wdb, wga, wgb, wgc) = _split_cols(w_in_l, sizes)
    w_main = jnp.concatenate([wq, wk, wv, wr, wu, wz, wxbc, wga, wgb, wgc], axis=1).astype(MXU_DTYPE)
    hp = HEADS_PER_GROUP
    pad = lambda w, width: jnp.pad(w, ((0, 0), (0, width - w.shape[1])))
    small = [pad(jnp.concatenate([wlf, wlb], axis=1), LANES)]
    for g in range(SSD_GROUPS):
        small.append(pad(jnp.concatenate([wdf[:, g * hp:(g + 1) * hp], wdb[:, g * hp:(g + 1) * hp]], axis=1), LANES))
    w_small = jnp.concatenate(small, axis=1).astype(MXU_DTYPE)
    aw_pad = jnp.zeros((2, LANES, qk), F32)
    aw_pad = aw_pad.at[0, :GLA_RANK].set(gla_a_w_l[0]).at[1, GLA_RANK:2 * GLA_RANK].set(gla_a_w_l[1])
    aw_pad = aw_pad.astype(MXU_DTYPE)
    ab = gla_a_b_l.reshape(2, 1, qk)
    grp = lambda v: v.reshape(SSD_GROUPS, hp)
    brow = jnp.concatenate([grp(dt_bias_l[0]), grp(dt_bias_l[1]), jnp.zeros((SSD_GROUPS, LANES - 2 * hp), F32)], axis=1)
    arow = jnp.concatenate([grp(a_log_l[0]), grp(a_log_l[1]), jnp.zeros((SSD_GROUPS, LANES - 2 * hp), F32)], axis=1)
    d_exp = jnp.repeat(ssd_d_l, SSD_HEADDIM).reshape(SSD_GROUPS, 1, hp * SSD_HEADDIM)
    return w_main, w_small, aw_pad, ab, brow.reshape(SSD_GROUPS, 1, LANES), arow.reshape(SSD_GROUPS, 1, LANES), d_exp


def kernel(x, c, ctx, c_ctx, ada_w, ada_b, norm1_g, norm2_g, w_in, gla_a_w, gla_a_b, gla_norm_g, gla_proj,
           fnet_proj, ssd_conv_w, ssd_conv_b, ssd_dt_bias, ssd_a_log, ssd_d, ssd_norm_g, ssd_proj, w_out,
           router_w, router_b, exp_w1, exp_w3, exp_w2, final_g):
    bsz, t_lat, d = x.shape
    n_ctx = ctx.shape[1]
    depth = ada_w.shape[0]
    s = n_ctx + t_lat
    n = bsz * s
    nct = n_ctx // TM
    ns = d // LANES

    xs0 = jnp.concatenate([ctx, x + _grid_sincos(t_lat, d)], axis=1)

    rows = 16
    c_all = jnp.concatenate([c, c_ctx[None], jnp.zeros((rows - bsz - 1, d), F32)], axis=0)
    mods = _ada(c_all, ada_w, ada_b)
    mods = mods.reshape(depth, rows, N_MOD, d)
    mods = jnp.pad(mods, ((0, 0), (0, 0), (0, SUB - N_MOD), (0, 0)))
    modsel = jnp.stack([jnp.broadcast_to(mods[:, bsz:bsz + 1], (depth, bsz, SUB, d)), mods[:, :bsz]], axis=2)

    width = d
    gd = width // FNET_GROUPS
    cc, cs_ = _dft_mats(gd, gd ** -0.5)
    w_cs = jnp.concatenate([cc, cs_], axis=1).astype(MXU_DTYPE)
    cos_l, sin_l = (m.astype(MXU_DTYPE) for m in _dft_mats(t_lat, t_lat ** -0.5))
    cos_c, sin_c = (m.astype(MXU_DTYPE) for m in _dft_mats(n_ctx, n_ctx ** -0.5))

    rw_t = router_w.T
    rb = router_b.reshape(N_EXPERTS, 1)
    lo_tab = jnp.array([4 * (bk // N_PAIRS) + PAIR_LO[bk % N_PAIRS] for bk in range(N_BUCKETS)], I32)
    hi_tab = jnp.array([4 * (bk // N_PAIRS) + PAIR_HI[bk % N_PAIRS] for bk in range(N_BUCKETS)], I32)
    cap_rows = n + N_BUCKETS * TMX
    ntile = cap_rows // TMX

    xcur = xs0
    h = _normmod(xcur, norm1_g[0], modsel[0], nct)
    out = None
    for l in range(depth):
        w_main, w_small, aw_pad, ab, brow, arow, d_exp = _layer_weights(
            d, w_in[l], gla_a_w[l], gla_a_b[l], ssd_dt_bias[l], ssd_a_log[l], ssd_d[l])
        p, ps = _inproj(h, w_main, w_small)
        xbc = _conv(p, ssd_conv_w[l], ssd_conv_b[l], n_ctx)
        ga = _gla(p, ps, aw_pad, ab, gla_norm_g[l], n_ctx)
        yz = _ssd(p, xbc, ps, brow, arow, d_exp, n_ctx)
        v = _chan_dft(p, w_cs, width)
        fm = _pos_dft(v, cos_c, sin_c, cos_l, sin_l, width)
        xmid, ht, ri, rf, wcol, cnt = _merge(
            ga, fm, yz, p, xcur, modsel[l], norm2_g[l].reshape(1, d), ssd_norm_g[l].reshape(1, -1),
            gla_proj[l].astype(MXU_DTYPE), fnet_proj[l].astype(MXU_DTYPE), ssd_proj[l].astype(MXU_DTYPE),
            w_out[l].astype(MXU_DTYPE), rw_t, rb, nct)

        counts = cnt[:N_BUCKETS, 0]
        tiles_per = (counts + TMX - 1) // TMX
        tile_end = jnp.cumsum(tiles_per)
        tile_start = tile_end - tiles_per
        bucket = ri[0]
        pos = (jnp.take(tile_start, bucket) * TMX + ri[1]).reshape(1, n)
        tile_bucket = jnp.minimum(jnp.searchsorted(tile_end, jnp.arange(ntile, dtype=I32), side="right"),
                                  N_BUCKETS - 1).astype(I32)
        tile_a = jnp.take(lo_tab, tile_bucket)
        tile_b = jnp.take(hi_tab, tile_bucket)
        nvalid = tile_end[-1:].astype(I32)

        xs_sorted = _dispatch(pos, ht, jnp.zeros((cap_rows * ns, LANES), F32), ns)
        w13 = jnp.concatenate([exp_w1[l], exp_w3[l]], axis=2).astype(MXU_DTYPE)
        ys = _experts(tile_a, tile_b, nvalid, xs_sorted, w13, exp_w2[l].astype(MXU_DTYPE), d)
        final = l == depth - 1
        if final:
            xcur, out = _combine(pos, ys, xmid, wcol, modsel[l], modsel[l], final_g, nct, True)
        else:
            xcur, h = _combine(pos, ys, xmid, wcol, modsel[l], modsel[l + 1], norm1_g[l + 1], nct, False)
    return out
```

```python
import functools
import math

import jax
import jax.numpy as jnp
from jax import lax
from jax.experimental import pallas as pl
from jax.experimental.pallas import tpu as pltpu

F32 = jnp.float32
BF16 = jnp.bfloat16
I32 = jnp.int32
MXU_DTYPE = BF16

EPS = 1e-6
GRID_W = 64
N_MOD = 6

GLA_HEADS = 4
GLA_DK = 128
GLA_DV = 256
GLA_RANK = 16
GLA_TAU = 16.0
GLA_CHUNK = 64
GLA_HPS = 2
GLA_BLOCK = 256

FNET_GROUPS = 4

SSD_HEADDIM = 64
SSD_GROUPS = 4
SSD_STATE = 128
SSD_CONV = 5
SSD_CHUNK = 128
HEADS_PER_GROUP = 8

N_EXPERTS = 16
N_EXPERT_GROUPS = 4
EXPERTS_PER_GROUP = 4
D_FF = 512
N_PAIRS = 6
N_BUCKETS = N_EXPERT_GROUPS * N_PAIRS
PAIR_LO = (0, 0, 0, 1, 1, 2)
PAIR_HI = (1, 2, 3, 2, 3, 3)

TM = 256
TMX = 256
LANES = 128
SUB = 8
VMEM_LIMIT = 56 * 1024 * 1024

COL_Q, COL_K, COL_V, COL_R, COL_U, COL_Z, COL_X, COL_B, COL_C, COL_GA, COL_GB, COL_GC = (
    0, 512, 1024, 2048, 3072, 4096, 6144, 8192, 8704, 9216, 10240, 11264)
P_MAIN = 12288
P_SMALL = LANES * (1 + SSD_GROUPS)
IN_TN = 512


def _cparams(sem, vmem=VMEM_LIMIT):
    return pltpu.CompilerParams(dimension_semantics=sem, vmem_limit_bytes=vmem)


def _mm(a, b):
    return jnp.dot(a.astype(MXU_DTYPE), b.astype(MXU_DTYPE), preferred_element_type=F32)


def _mm_nt(a, b):
    return lax.dot_general(a.astype(MXU_DTYPE), b.astype(MXU_DTYPE), (((1,), (1,)), ((), ())),
                           preferred_element_type=F32)


def _mm_tn(a, b):
    return lax.dot_general(a.astype(MXU_DTYPE), b.astype(MXU_DTYPE), (((0,), (0,)), ((), ())),
                           preferred_element_type=F32)


def _split3(x):
    hi = x.astype(MXU_DTYPE)
    r1 = x - hi.astype(F32)
    mid = r1.astype(MXU_DTYPE)
    lo = (r1 - mid.astype(F32)).astype(MXU_DTYPE)
    return hi, mid, lo


def _mm_exact_lhs(m01, x):
    hi, mid, lo = _split3(x)
    m = m01.astype(MXU_DTYPE)
    return (jnp.dot(m, hi, preferred_element_type=F32) + jnp.dot(m, mid, preferred_element_type=F32)
            + jnp.dot(m, lo, preferred_element_type=F32))


def _sigmoid(x):
    return 1.0 / (1.0 + jnp.exp(-x))


def _silu(x):
    return x * _sigmoid(x)


def _softplus(x):
    return jnp.maximum(x, 0.0) + jnp.log(1.0 + jnp.exp(-jnp.abs(x)))


def _rms(x, g):
    return x * lax.rsqrt(jnp.mean(x * x, axis=-1, keepdims=True) + EPS) * g


def _modulate(x, g, shift, scale):
    return _rms(x, g) * (1.0 + scale) + shift


def _ada_kernel(c_ref, w_ref, b_ref, o_ref):
    o_ref[0] = _mm(_silu(c_ref[...]), w_ref[0]) + b_ref[0]


def _ada(c_all, ada_w, ada_b):
    depth, d, nd = ada_w.shape
    rows = c_all.shape[0]
    tn = 1024
    return pl.pallas_call(
        _ada_kernel,
        grid=(depth, nd // tn),
        in_specs=[pl.BlockSpec((rows, d), lambda l, j: (0, 0)),
                  pl.BlockSpec((1, d, tn), lambda l, j: (l, 0, j)),
                  pl.BlockSpec((1, 1, tn), lambda l, j: (l, 0, j))],
        out_specs=pl.BlockSpec((1, rows, tn), lambda l, j: (l, 0, j)),
        out_shape=jax.ShapeDtypeStruct((depth, rows, nd), F32),
        compiler_params=_cparams(("arbitrary", "arbitrary")),
        name="ada_mod",
    )(c_all, ada_w, ada_b.reshape(depth, 1, nd))


def _normmod_kernel(x_ref, g_ref, mod_ref, h_ref):
    mod = mod_ref[0, 0]
    h_ref[0] = _modulate(x_ref[0], g_ref[...], mod[0:1], mod[1:2]).astype(h_ref.dtype)


def _normmod(x, g, modsel, n_ctx_tiles):
    b, s, d = x.shape
    return pl.pallas_call(
        _normmod_kernel,
        grid=(b, s // TM),
        in_specs=[pl.BlockSpec((1, TM, d), lambda i, j: (i, j, 0)),
                  pl.BlockSpec((1, d), lambda i, j: (0, 0)),
                  pl.BlockSpec((1, 1, SUB, d), lambda i, j: (i, jnp.where(j < n_ctx_tiles, 0, 1), 0, 0))],
        out_specs=pl.BlockSpec((1, TM, d), lambda i, j: (i, j, 0)),
        out_shape=jax.ShapeDtypeStruct((b, s, d), MXU_DTYPE),
        compiler_params=_cparams(("arbitrary", "arbitrary")),
        name="norm_mod",
    )(x, g.reshape(1, d), modsel)


def _inproj_kernel(h_ref, w_ref, ws_ref, p_ref, ps_ref):
    h = h_ref[0]
    p_ref[0] = _mm(h, w_ref[...]).astype(p_ref.dtype)

    @pl.when(pl.program_id(1) == 0)
    def _():
        ps_ref[0] = _mm(h, ws_ref[...])


def _inproj(h, w_main, w_small):
    b, s, d = h.shape
    return pl.pallas_call(
        _inproj_kernel,
        grid=(b, P_MAIN // IN_TN),
        in_specs=[pl.BlockSpec((1, s, d), lambda i, j: (i, 0, 0)),
                  pl.BlockSpec((d, IN_TN), lambda i, j: (0, j)),
                  pl.BlockSpec((d, P_SMALL), lambda i, j: (0, 0))],
        out_specs=[pl.BlockSpec((1, s, IN_TN), lambda i, j: (i, 0, j)),
                   pl.BlockSpec((1, s, P_SMALL), lambda i, j: (i, 0, 0))],
        out_shape=[jax.ShapeDtypeStruct((b, s, P_MAIN), MXU_DTYPE),
                   jax.ShapeDtypeStruct((b, s, P_SMALL), F32)],
        compiler_params=_cparams(("arbitrary", "arbitrary")),
        name="in_proj",
    )(h, w_main, w_small)


def _conv_kernel(p_ref, w_ref, b_ref, o_ref, *, n_ctx):
    x = p_ref[0].astype(F32)
    s = x.shape[0]
    w = w_ref[...]
    row = lax.broadcasted_iota(I32, (s, 1), 0)
    seg_lo = jnp.where(row < n_ctx, 0, n_ctx)
    seg_hi = jnp.where(row < n_ctx, n_ctx, s)
    acc = jnp.zeros_like(x) + b_ref[...]
    half = SSD_CONV // 2
    for k in range(SSD_CONV):
        dlt = k - half
        xs = x if dlt == 0 else pltpu.roll(x, shift=(-dlt) % s, axis=0)
        t2 = row + dlt
        ok = jnp.broadcast_to(jnp.where((t2 >= seg_lo) & (t2 < seg_hi), 1.0, 0.0), x.shape)
        acc = acc + w[k:k + 1] * jnp.where(ok > 0.5, xs, 0.0)
    o_ref[0] = _silu(acc).astype(o_ref.dtype)


def _conv(p, conv_w, conv_b, n_ctx):
    b, s, _ = p.shape
    c = conv_w.shape[1]
    tc = 256
    return pl.pallas_call(
        functools.partial(_conv_kernel, n_ctx=n_ctx),
        grid=(b, c // tc),
        in_specs=[pl.BlockSpec((1, s, tc), lambda i, j: (i, 0, COL_X // tc + j)),
                  pl.BlockSpec((SSD_CONV, tc), lambda i, j: (0, j)),
                  pl.BlockSpec((1, tc), lambda i, j: (0, j))],
        out_specs=pl.BlockSpec((1, s, tc), lambda i, j: (i, 0, j)),
        out_shape=jax.ShapeDtypeStruct((b, s, c), MXU_DTYPE),
        compiler_params=_cparams(("arbitrary", "arbitrary")),
        name="ssd_conv",
    )(p, conv_w, conv_b.reshape(1, c))


def _gla_kernel(q_ref, k_ref, v_ref, r_ref, ps_ref, aw_ref, ab_ref, g_ref, o_ref,
                la_s, od_s, st_s, *, n_ctx):
    L = GLA_CHUNK
    R = GLA_BLOCK
    cpb = R // L
    dk, dv = GLA_DK, GLA_DV
    s = q_ref.shape[1]
    nb = s // R
    nbc = n_ctx // R
    lr = ps_ref[0]
    for d in range(2):
        pre = _mm(lr, aw_ref[d]) + ab_ref[d]
        la_s[d] = -_softplus(-pre) * (1.0 / GLA_TAU)

    ri = lax.broadcasted_iota(I32, (R, R), 0)
    ci = lax.broadcasted_iota(I32, (R, R), 1)
    same = (ri // L) == (ci // L)
    scale = dk ** -0.5
    st_s[...] = jnp.zeros_like(st_s)

    def per_chunk(x, idx):
        x3 = x.reshape(cpb, L, x.shape[-1])
        return jnp.broadcast_to(x3[:, idx:idx + 1, :], x3.shape).reshape(x.shape)

    def block(off, hh, reverse, st, keep, bcum_all):
        q = q_ref[0, pl.ds(off, R), hh * dk:(hh + 1) * dk].astype(F32) * scale
        k = k_ref[0, pl.ds(off, R), hh * dk:(hh + 1) * dk].astype(F32)
        v = v_ref[0, pl.ds(off, R), hh * dv:(hh + 1) * dv]
        bcum = bcum_all[:, hh * dk:(hh + 1) * dk]
        b_mid = per_chunk(bcum, L // 2 if reverse else L // 2 - 1)
        b_last = per_chunk(bcum, 0 if reverse else L - 1)
        sc = _mm_nt(q * jnp.exp(bcum - b_mid), k * jnp.exp(b_mid - bcum))
        o_intra = _mm(jnp.where(keep, sc, 0.0), v)
        qb = (q * jnp.exp(bcum)).astype(MXU_DTYPE)
        kd = (k * jnp.exp(b_last - bcum)).astype(MXU_DTYPE)
        dec = jnp.exp(b_last)
        o_inter = [None] * cpb
        for c in (range(cpb - 1, -1, -1) if reverse else range(cpb)):
            rows = slice(c * L, (c + 1) * L)
            o_inter[c] = _mm_nt(qb[rows], st)
            st = dec[c * L:c * L + 1] * st + _mm_tn(v[rows], kd[rows])
        return o_intra + jnp.concatenate(o_inter, axis=0), st

    @pl.loop(0, nb)
    def _(i):
        off_f = pl.multiple_of(i * R, R)
        jb = jnp.where(i < nbc, nbc - 1 - i, nb + nbc - 1 - i)
        off_b = pl.multiple_of(jb * R, R)
        offs = (off_f, off_b)
        keeps = (same & (ci <= ri), same & (ci >= ri))
        bcums = [_mm_exact_lhs(jnp.where(keeps[d], 1.0, 0.0), la_s[d, pl.ds(offs[d], R), :]) for d in range(2)]
        chains = [(d, hh) for hh in range(GLA_HPS) for d in range(2)]
        states = [st_s[d, hh] for d, hh in chains]
        res = [block(offs[d], hh, bool(d), st, keeps[d], bcums[d]) for (d, hh), st in zip(chains, states)]
        for (d, hh), (o, st) in zip(chains, res):
            od_s[d, pl.ds(off_b if d else off_f, R), hh * dv:(hh + 1) * dv] = o
            st_s[d, hh] = st

    g = g_ref[...]

    @pl.loop(0, s // L)
    def _(c):
        off = pl.multiple_of(c * L, L)
        o = od_s[0, pl.ds(off, L), :] + od_s[1, pl.ds(off, L), :]
        r = r_ref[0, pl.ds(off, L), :].astype(F32)
        outs = [_rms(o[:, hh * dv:(hh + 1) * dv], g) for hh in range(GLA_HPS)]
        o_ref[0, pl.ds(off, L), :] = (jnp.concatenate(outs, axis=1) * _silu(r)).astype(o_ref.dtype)


def _gla(p, ps, aw_pad, ab, norm_g, n_ctx):
    b, s, _ = p.shape
    dk, dv = GLA_HPS * GLA_DK, GLA_HPS * GLA_DV
    return pl.pallas_call(
        functools.partial(_gla_kernel, n_ctx=n_ctx),
        grid=(b, GLA_HEADS // GLA_HPS),
        in_specs=[pl.BlockSpec((1, s, dk), lambda i, h: (i, 0, COL_Q // dk + h)),
                  pl.BlockSpec((1, s, dk), lambda i, h: (i, 0, COL_K // dk + h)),
                  pl.BlockSpec((1, s, dv), lambda i, h: (i, 0, COL_V // dv + h)),
                  pl.BlockSpec((1, s, dv), lambda i, h: (i, 0, COL_R // dv + h)),
                  pl.BlockSpec((1, s, LANES), lambda i, h: (i, 0, 0)),
                  pl.BlockSpec((2, LANES, dk), lambda i, h: (0, 0, h)),
                  pl.BlockSpec((2, 1, dk), lambda i, h: (0, 0, h)),
                  pl.BlockSpec((1, GLA_DV), lambda i, h: (0, 0))],
        out_specs=pl.BlockSpec((1, s, dv), lambda i, h: (i, 0, h)),
        out_shape=jax.ShapeDtypeStruct((b, s, GLA_HEADS * GLA_DV), MXU_DTYPE),
        scratch_shapes=[pltpu.VMEM((2, s, dk), F32), pltpu.VMEM((2, s, dv), F32),
                        pltpu.VMEM((2, GLA_HPS, GLA_DV, GLA_DK), F32)],
        compiler_params=_cparams(("arbitrary", "arbitrary")),
        name="gla",
    )(p, p, p, p, ps, aw_pad, ab, norm_g.reshape(1, GLA_DV))


def _ssd_kernel(z_ref, x_ref, bm_ref, cm_ref, ps_ref, brow_ref, alog_ref, d_ref, o_ref,
                dt_s, w_s, yacc_s, st_s, *, n_ctx):
    L = SSD_CHUNK
    s = x_ref.shape[1]
    n = s // L
    nc = n_ctx // L
    hp = HEADS_PER_GROUP
    dtall = _softplus(ps_ref[0] + brow_ref[0])
    dt_s[...] = dtall
    w_s[...] = dtall * (-jnp.exp(alog_ref[0]))

    ri = lax.broadcasted_iota(I32, (L, L), 0)
    ci = lax.broadcasted_iota(I32, (L, L), 1)
    lane = lax.broadcasted_iota(I32, (1, LANES), 1)
    low = lane < SSD_HEADDIM
    npair = hp // 2

    def expand(cols, base):
        blocks = []
        for pr in range(npair):
            a = cols[:, base + 2 * pr:base + 2 * pr + 1]
            b = cols[:, base + 2 * pr + 1:base + 2 * pr + 2]
            blocks.append(jnp.where(low, a, b))
        return jnp.concatenate(blocks, axis=1)

    def chunk(off, reverse):
        base = hp if reverse else 0
        keep = (ci >= ri) if reverse else (ci <= ri)
        x = x_ref[0, pl.ds(off, L), :].astype(F32)
        bm = bm_ref[0, pl.ds(off, L), :]
        cm = cm_ref[0, pl.ds(off, L), :]
        dtc = dt_s[pl.ds(off, L), :]
        wc = w_s[pl.ds(off, L), :]
        acs = _mm_exact_lhs(jnp.where(keep, 1.0, 0.0), wc)
        acs_t = acs.T
        i_last = 0 if reverse else L - 1
        xdt = x * expand(dtc, base)
        acs_e = expand(acs, base)
        acs_last = acs_e[i_last:i_last + 1]
        cb = _mm_nt(cm, bm)
        st = st_s[...]
        y_off = _mm(cm, st) * jnp.exp(acs_e)
        ys = []
        for pr in range(npair):
            ms = []
            for j in (2 * pr, 2 * pr + 1):
                sg = acs[:, base + j:base + j + 1] - acs_t[base + j:base + j + 1, :]
                ms.append(cb * jnp.exp(jnp.where(keep, sg, -1e30)))
            m2 = jnp.concatenate(ms, axis=1)
            xp = xdt[:, pr * LANES:(pr + 1) * LANES]
            bd = jnp.concatenate([jnp.where(low, xp, 0.0), jnp.where(low, 0.0, xp)], axis=0)
            ys.append(_mm(m2, bd))
        y = jnp.concatenate(ys, axis=1) + y_off
        st_s[...] = jnp.exp(acs_last) * st + _mm_tn(bm, xdt * jnp.exp(acs_last - acs_e))
        return y, x

    st_s[...] = jnp.zeros_like(st_s)

    @pl.loop(0, n)
    def _(c):
        off = pl.multiple_of(c * L, L)
        y, _x = chunk(off, False)
        yacc_s[pl.ds(off, L), :] = y

    st_s[...] = jnp.zeros_like(st_s)
    dsk = d_ref[0]

    @pl.loop(0, n)
    def _(i):
        c = jnp.where(i < nc, nc - 1 - i, n + nc - 1 - i)
        off = pl.multiple_of(c * L, L)
        y, x = chunk(off, True)
        y = y + yacc_s[pl.ds(off, L), :] + dsk * x
        z = z_ref[0, pl.ds(off, L), :].astype(F32)
        o_ref[0, pl.ds(off, L), :] = (y * _silu(z)).astype(o_ref.dtype)


def _ssd(p, xbc, ps, brow, alog_row, d_exp, n_ctx):
    b, s, _ = p.shape
    gw = HEADS_PER_GROUP * SSD_HEADDIM
    inner = SSD_GROUPS * gw
    ns = SSD_STATE
    return pl.pallas_call(
        functools.partial(_ssd_kernel, n_ctx=n_ctx),
        grid=(b, SSD_GROUPS),
        in_specs=[pl.BlockSpec((1, s, gw), lambda i, g: (i, 0, COL_Z // gw + g)),
                  pl.BlockSpec((1, s, gw), lambda i, g: (i, 0, g)),
                  pl.BlockSpec((1, s, ns), lambda i, g: (i, 0, inner // ns + g)),
                  pl.BlockSpec((1, s, ns), lambda i, g: (i, 0, inner // ns + SSD_GROUPS + g)),
                  pl.BlockSpec((1, s, LANES), lambda i, g: (i, 0, 1 + g)),
                  pl.BlockSpec((1, 1, LANES), lambda i, g: (g, 0, 0)),
                  pl.BlockSpec((1, 1, LANES), lambda i, g: (g, 0, 0)),
                  pl.BlockSpec((1, 1, gw), lambda i, g: (g, 0, 0))],
        out_specs=pl.BlockSpec((1, s, gw), lambda i, g: (i, 0, g)),
        out_shape=jax.ShapeDtypeStruct((b, s, inner), MXU_DTYPE),
        scratch_shapes=[pltpu.VMEM((s, LANES), F32), pltpu.VMEM((s, LANES), F32),
                        pltpu.VMEM((s, gw), F32), pltpu.VMEM((ns, gw), F32)],
        compiler_params=_cparams(("arbitrary", "arbitrary")),
        name="ssd",
    )(p, xbc, xbc, xbc, ps, brow, alog_row, d_exp)


def _chan_dft_kernel(u_ref, w_ref, o_ref):
    u = u_ref[0]
    gd = w_ref.shape[0]
    w = w_ref[...]
    cs, sn = [], []
    for g in range(FNET_GROUPS):
        r = _mm(u[:, g * gd:(g + 1) * gd], w)
        cs.append(r[:, :gd])
        sn.append(r[:, gd:])
    o_ref[0] = jnp.concatenate(cs + sn, axis=1).astype(o_ref.dtype)


def _chan_dft(p, w_cs, width):
    b, s, _ = p.shape
    gd = width // FNET_GROUPS
    return pl.pallas_call(
        _chan_dft_kernel,
        grid=(b, s // TM),
        in_specs=[pl.BlockSpec((1, TM, width), lambda i, j: (i, j, COL_U // width)),
                  pl.BlockSpec((gd, 2 * gd), lambda i, j: (0, 0))],
        out_specs=pl.BlockSpec((1, TM, 2 * width), lambda i, j: (i, j, 0)),
        out_shape=jax.ShapeDtypeStruct((b, s, 2 * width), MXU_DTYPE),
        compiler_params=_cparams(("arbitrary", "arbitrary")),
        name="fnet_chan",
    )(p, w_cs)


def _pos_dft_kernel(cc_ref, sc_ref, cl_ref, sl_ref, uc_ref, us_ref, o_ref, *, n_ctx):
    j = pl.program_id(1)
    nct = n_ctx // TM

    @pl.when(j < nct)
    def _():
        o_ref[0] = (_mm(cc_ref[...], uc_ref[0, :n_ctx, :]) - _mm(sc_ref[...], us_ref[0, :n_ctx, :])).astype(o_ref.dtype)

    @pl.when(j >= nct)
    def _():
        o_ref[0] = (_mm(cl_ref[...], uc_ref[0, n_ctx:, :]) - _mm(sl_ref[...], us_ref[0, n_ctx:, :])).astype(o_ref.dtype)


def _pos_dft(v, cos_c, sin_c, cos_l, sin_l, width):
    b, s, _ = v.shape
    n_ctx = cos_c.shape[0]
    t = cos_l.shape[0]
    nct = n_ctx // TM
    cidx = lambda i, j: (jnp.minimum(j, nct - 1), 0)
    lidx = lambda i, j: (jnp.maximum(j - nct, 0), 0)
    return pl.pallas_call(
        functools.partial(_pos_dft_kernel, n_ctx=n_ctx),
        grid=(b, s // TM),
        in_specs=[pl.BlockSpec((TM, n_ctx), cidx),
                  pl.BlockSpec((TM, n_ctx), cidx),
                  pl.BlockSpec((TM, t), lidx),
                  pl.BlockSpec((TM, t), lidx),
                  pl.BlockSpec((1, s, width), lambda i, j: (i, 0, 0)),
                  pl.BlockSpec((1, s, width), lambda i, j: (i, 0, 1))],
        out_specs=pl.BlockSpec((1, TM, width), lambda i, j: (i, j, 0)),
        out_shape=jax.ShapeDtypeStruct((b, s, width), MXU_DTYPE),
        compiler_params=_cparams(("arbitrary", "arbitrary")),
        name="fnet_pos",
    )(cos_c, sin_c, cos_l, sin_l, v, v)


def _merge_kernel(ga_ref, fm_ref, yz_ref, g1_ref, g2_ref, g3_ref, x_ref, mod_ref, n2g_ref, sg_ref,
                  wa_ref, wb_ref, wc_ref, wo_ref, rw_ref, rb_ref,
                  xo_ref, ht_ref, ri_ref, rf_ref, wcol_ref, cnt_ref, carry_s):
    first = (pl.program_id(0) == 0) & (pl.program_id(1) == 0)

    @pl.when(first)
    def _():
        carry_s[...] = jnp.zeros_like(carry_s)

    mod = mod_ref[0, 0]
    ya = _mm(ga_ref[0], wa_ref[...])
    yb = _mm(fm_ref[0], wb_ref[...])
    yz = yz_ref[0].astype(F32)
    yc = _mm(_rms(yz, sg_ref[...]), wc_ref[...])
    y = (_sigmoid(g1_ref[0].astype(F32)) * ya + _sigmoid(g2_ref[0].astype(F32)) * yb
         + _sigmoid(g3_ref[0].astype(F32)) * yc)
    xn = x_ref[0] + mod[2:3] * _mm(y, wo_ref[...])
    xo_ref[0] = xn
    h2 = _modulate(xn, n2g_ref[...], mod[3:4], mod[4:5])
    d = h2.shape[1]
    tm = h2.shape[0]
    for sidx in range(d // LANES):
        ht_ref[pl.ds(sidx, tm, stride=d // LANES), :] = h2[:, sidx * LANES:(sidx + 1) * LANES]

    hh, hm, _hl = _split3(h2)
    rh, rm, _rl = _split3(rw_ref[...])
    nt = (((1,), (1,)), ((), ()))
    lg = (lax.dot_general(rh, hh, nt, preferred_element_type=F32)
          + lax.dot_general(rh, hm, nt, preferred_element_type=F32)
          + lax.dot_general(rm, hh, nt, preferred_element_type=F32))
    sc = _sigmoid(lg)
    sel = sc + rb_ref[...]
    srow = [sel[e:e + 1] for e in range(N_EXPERTS)]
    urow = [sc[e:e + 1] for e in range(N_EXPERTS)]
    epg = EXPERTS_PER_GROUP

    best_v = best_pair = best_g = best_slo = best_shi = None
    for g in range(N_EXPERT_GROUPS):
        a = srow[g * epg:(g + 1) * epg]
        u = urow[g * epg:(g + 1) * epg]
        gs = None
        for i in range(epg):
            for j in range(i + 1, epg):
                ps_ = a[i] + a[j]
                gs = ps_ if gs is None else jnp.maximum(gs, ps_)
        m1, i1 = a[0], jnp.zeros_like(a[0], dtype=I32)
        for i in range(1, epg):
            up = a[i] > m1
            i1 = jnp.where(up, i, i1)
            m1 = jnp.where(up, a[i], m1)
        m2 = jnp.full_like(a[0], -jnp.inf)
        i2 = jnp.full_like(i1, -1)
        for i in range(epg):
            cand = (i1 != i) & (a[i] > m2)
            i2 = jnp.where(cand, i, i2)
            m2 = jnp.where(cand, a[i], m2)
        lo = jnp.minimum(i1, i2)
        hi = jnp.maximum(i1, i2)
        pair = jnp.where(lo == 0, 0, jnp.where(lo == 1, 3, 5)) + (hi - lo - 1)
        s_lo = jnp.where(lo == 0, u[0], jnp.where(lo == 1, u[1], u[2]))
        s_hi = jnp.where(hi == 1, u[1], jnp.where(hi == 2, u[2], u[3]))
        if g == 0:
            best_v, best_pair, best_g, best_slo, best_shi = gs, pair, jnp.zeros_like(pair), s_lo, s_hi
        else:
            up = gs > best_v
            best_v = jnp.where(up, gs, best_v)
            best_pair = jnp.where(up, pair, best_pair)
            best_g = jnp.where(up, g, best_g)
            best_slo = jnp.where(up, s_lo, best_slo)
            best_shi = jnp.where(up, s_hi, best_shi)
    bucket = best_g * N_PAIRS + best_pair
    tot = best_slo + best_shi
    w_lo = best_slo / tot
    w_hi = best_shi / tot

    nb = carry_s.shape[0]
    oh = (lax.broadcasted_iota(I32, (nb, tm), 0) == bucket).astype(F32)
    tri = (lax.broadcasted_iota(I32, (tm, tm), 0) <= lax.broadcasted_iota(I32, (tm, tm), 1))
    csum = _mm(oh, jnp.where(tri, 1.0, 0.0))
    carry = carry_s[...]
    rank = jnp.sum(oh * (csum - 1.0 + carry), axis=0, keepdims=True)
    carry = carry + jnp.sum(oh, axis=1, keepdims=True)
    carry_s[...] = carry
    cnt_ref[...] = jnp.broadcast_to(carry, cnt_ref.shape).astype(I32)

    zi = jnp.zeros((SUB - 2, tm), I32)
    ri_ref[...] = jnp.concatenate([bucket, rank.astype(I32), zi], axis=0)
    wrows = jnp.concatenate([w_lo, w_hi, jnp.zeros((SUB - 2, tm), F32)], axis=0)
    rf_ref[...] = wrows
    eye = (lax.broadcasted_iota(I32, (SUB, LANES), 0) == lax.broadcasted_iota(I32, (SUB, LANES), 1))
    eye = jnp.where(eye, 1.0, 0.0).astype(MXU_DTYPE)
    tn = (((0,), (0,)), ((), ()))
    p1, p2, p3 = _split3(wrows)
    wcol_ref[...] = (lax.dot_general(p1, eye, tn, preferred_element_type=F32)
                     + lax.dot_general(p2, eye, tn, preferred_element_type=F32)
                     + lax.dot_general(p3, eye, tn, preferred_element_type=F32))


def _merge(ga, fm, yz, p, x, modsel, n2g, ssd_g, w_a, w_b, w_c, w_o, rw_t, rb, n_ctx_tiles):
    b, s, d = x.shape
    nt = s // TM
    n = b * s
    di = yz.shape[2]
    nbp = 32
    tok = lambda i, j: (i, j, 0)
    flat = lambda i, j: (0, i * nt + j)
    const = lambda i, j: (0, 0)
    return pl.pallas_call(
        _merge_kernel,
        grid=(b, nt),
        in_specs=[pl.BlockSpec((1, TM, d), tok),
                  pl.BlockSpec((1, TM, d), tok),
                  pl.BlockSpec((1, TM, di), tok),
                  pl.BlockSpec((1, TM, d), lambda i, j: (i, j, COL_GA // d)),
                  pl.BlockSpec((1, TM, d), lambda i, j: (i, j, COL_GB // d)),
                  pl.BlockSpec((1, TM, d), lambda i, j: (i, j, COL_GC // d)),
                  pl.BlockSpec((1, TM, d), tok),
                  pl.BlockSpec((1, 1, SUB, d), lambda i, j: (i, jnp.where(j < n_ctx_tiles, 0, 1), 0, 0)),
                  pl.BlockSpec((1, d), const),
                  pl.BlockSpec((1, di), const),
                  pl.BlockSpec((d, d), const),
                  pl.BlockSpec((d, d), const),
                  pl.BlockSpec((di, d), const),
                  pl.BlockSpec((d, d), const),
                  pl.BlockSpec((N_EXPERTS, d), const),
                  pl.BlockSpec((N_EXPERTS, 1), const)],
        out_specs=[pl.BlockSpec((1, TM, d), tok),
                   pl.BlockSpec((TM * (d // LANES), LANES), lambda i, j: (i * nt + j, 0)),
                   pl.BlockSpec((SUB, TM), flat),
                   pl.BlockSpec((SUB, TM), flat),
                   pl.BlockSpec((TM, LANES), lambda i, j: (i * nt + j, 0)),
                   pl.BlockSpec((nbp, LANES), const)],
        out_shape=[jax.ShapeDtypeStruct((b, s, d), F32),
                   jax.ShapeDtypeStruct((n * (d // LANES), LANES), F32),
                   jax.ShapeDtypeStruct((SUB, n), I32),
                   jax.ShapeDtypeStruct((SUB, n), F32),
                   jax.ShapeDtypeStruct((n, LANES), F32),
                   jax.ShapeDtypeStruct((nbp, LANES), I32)],
        scratch_shapes=[pltpu.VMEM((nbp, 1), F32)],
        compiler_params=_cparams(("arbitrary", "arbitrary")),
        name="merge_route",
    )(ga, fm, yz, p, p, p, x, modsel, n2g, ssd_g, w_a, w_b, w_c, w_o, rw_t, rb)


def _row_copy(src, dst, src_row, dst_row, rows, sem):
    def at(ref, r):
        start = r * rows if isinstance(r, int) else pl.multiple_of(r * rows, rows)
        return ref.at[pl.ds(start, rows)]
    return pltpu.make_async_copy(at(src, src_row), at(dst, dst_row), sem)


def _dispatch_kernel(pos_ref, ht_ref, xs_in_ref, xs_ref, sem, *, rows):
    del xs_in_ref
    copies = [_row_copy(ht_ref, xs_ref, t, pos_ref[0, t], rows, sem) for t in range(TM)]
    for cp in copies:
        cp.start()
    for cp in copies:
        cp.wait()


def _dispatch(pos, ht, xs_init, rows):
    n = pos.shape[1]
    return pl.pallas_call(
        functools.partial(_dispatch_kernel, rows=rows),
        grid=(n // TM,),
        in_specs=[pl.BlockSpec((1, TM), lambda i: (0, i), memory_space=pltpu.SMEM),
                  pl.BlockSpec((TM * rows, LANES), lambda i: (i, 0)),
                  pl.BlockSpec(memory_space=pl.ANY)],
        out_specs=pl.BlockSpec(memory_space=pl.ANY),
        out_shape=jax.ShapeDtypeStruct(xs_init.shape, xs_init.dtype),
        input_output_aliases={2: 0},
        scratch_shapes=[pltpu.SemaphoreType.DMA(())],
        compiler_params=_cparams(("arbitrary",)),
        name="moe_dispatch",
    )(pos, ht, xs_init)


def _expert_kernel(ta_ref, tb_ref, nv_ref, xs_ref, w13a_ref, w2a_ref, w13b_ref, w2b_ref, ys_ref):
    del ta_ref, tb_ref

    @pl.when(pl.program_id(0) >= nv_ref[0])
    def _():
        ys_ref[...] = jnp.zeros_like(ys_ref)

    @pl.when(pl.program_id(0) < nv_ref[0])
    def _():
        ns = xs_ref.shape[0] // TMX
        x = jnp.concatenate([xs_ref[pl.ds(sidx, TMX, stride=ns), :] for sidx in range(ns)], axis=1)
        x = x.astype(MXU_DTYPE)
        for kk, (w13_ref, w2_ref) in enumerate(((w13a_ref, w2a_ref), (w13b_ref, w2b_ref))):
            h13 = _mm(x, w13_ref[0])
            he = _silu(h13[:, :D_FF]) * h13[:, D_FF:]
            y = _mm(he, w2_ref[0])
            for sidx in range(ns):
                ys_ref[pl.ds(kk * ns + sidx, TMX, stride=2 * ns), :] = y[:, sidx * LANES:(sidx + 1) * LANES]


def _experts(tile_a, tile_b, nvalid, xs, w13, w2, d):
    ns = d // LANES
    ntile = xs.shape[0] // (TMX * ns)
    last = lambda i, nv: jnp.minimum(i, nv[0] - 1)
    grid_spec = pltpu.PrefetchScalarGridSpec(
        num_scalar_prefetch=3,
        grid=(ntile,),
        in_specs=[pl.BlockSpec((TMX * ns, LANES), lambda i, ta, tb, nv: (last(i, nv), 0)),
                  pl.BlockSpec((1, d, 2 * D_FF), lambda i, ta, tb, nv: (ta[last(i, nv)], 0, 0)),
                  pl.BlockSpec((1, D_FF, d), lambda i, ta, tb, nv: (ta[last(i, nv)], 0, 0)),
                  pl.BlockSpec((1, d, 2 * D_FF), lambda i, ta, tb, nv: (tb[last(i, nv)], 0, 0)),
                  pl.BlockSpec((1, D_FF, d), lambda i, ta, tb, nv: (tb[last(i, nv)], 0, 0))],
        out_specs=pl.BlockSpec((TMX * 2 * ns, LANES), lambda i, ta, tb, nv: (i, 0)),
    )
    return pl.pallas_call(
        _expert_kernel,
        grid_spec=grid_spec,
        out_shape=jax.ShapeDtypeStruct((ntile * TMX * 2 * ns, LANES), F32),
        compiler_params=_cparams(("arbitrary",)),
        name="moe_experts",
    )(tile_a, tile_b, nvalid, xs, w13, w2, w13, w2)


def _combine_kernel(pos_ref, ys_ref, x_ref, wcol_ref, mod_ref, modn_ref, g_ref, xo_ref, ho_ref, buf, sem,
                    *, rows, final):
    copies = [_row_copy(ys_ref, buf, pos_ref[0, t], t, rows, sem) for t in range(TM)]
    for cp in copies:
        cp.start()
    for cp in copies:
        cp.wait()

    ns = rows // 2
    ya = jnp.concatenate([buf[pl.ds(sidx, TM, stride=rows), :] for sidx in range(ns)], axis=1)
    yb = jnp.concatenate([buf[pl.ds(ns + sidx, TM, stride=rows), :] for sidx in range(ns)], axis=1)
    wc = wcol_ref[...]
    y = wc[:, 0:1] * ya + wc[:, 1:2] * yb
    mod = mod_ref[0, 0]
    xn = x_ref[0] + mod[5:6] * y
    xo_ref[0] = xn
    if final:
        ho_ref[0] = _rms(xn, g_ref[...]).astype(ho_ref.dtype)
    else:
        modn = modn_ref[0, 0]
        ho_ref[0] = _modulate(xn, g_ref[...], modn[0:1], modn[1:2]).astype(ho_ref.dtype)


def _combine(pos, ys, x, wcol, modsel, modsel_next, g_next, n_ctx_tiles, final):
    b, s, d = x.shape
    nt = s // TM
    rows = 2 * (d // LANES)
    tok = lambda i, j: (i, j, 0)
    msel = lambda i, j: (i, jnp.where(j < n_ctx_tiles, 0, 1), 0, 0)
    if final:
        t_lat = s - n_ctx_tiles * TM
        h_shape = jax.ShapeDtypeStruct((b, t_lat, d), F32)
        h_spec = pl.BlockSpec((1, TM, d), lambda i, j: (i, jnp.maximum(j - n_ctx_tiles, 0), 0))
    else:
        h_shape = jax.ShapeDtypeStruct((b, s, d), MXU_DTYPE)
        h_spec = pl.BlockSpec((1, TM, d), tok)
    return pl.pallas_call(
        functools.partial(_combine_kernel, rows=rows, final=final),
        grid=(b, nt),
        in_specs=[pl.BlockSpec((1, TM), lambda i, j: (0, i * nt + j), memory_space=pltpu.SMEM),
                  pl.BlockSpec(memory_space=pl.ANY),
                  pl.BlockSpec((1, TM, d), tok),
                  pl.BlockSpec((TM, LANES), lambda i, j: (i * nt + j, 0)),
                  pl.BlockSpec((1, 1, SUB, d), msel),
                  pl.BlockSpec((1, 1, SUB, d), msel),
                  pl.BlockSpec((1, d), lambda i, j: (0, 0))],
        out_specs=[pl.BlockSpec((1, TM, d), tok), h_spec],
        out_shape=[jax.ShapeDtypeStruct((b, s, d), F32), h_shape],
        scratch_shapes=[pltpu.VMEM((TM * rows, LANES), F32), pltpu.SemaphoreType.DMA(())],
        compiler_params=_cparams(("arbitrary", "arbitrary")),
        name="moe_combine_final" if final else "moe_combine",
    )(pos, ys, x, wcol, modsel, modsel_next, g_next.reshape(1, d))


def _grid_sincos(n_tokens, dim):
    rows = n_tokens // GRID_W
    row = jnp.broadcast_to(jnp.arange(rows)[:, None], (rows, GRID_W)).reshape(-1).astype(F32)
    col = jnp.broadcast_to(jnp.arange(GRID_W)[None, :], (rows, GRID_W)).reshape(-1).astype(F32)
    quarter = dim // 4
    freqs = jnp.exp(-math.log(10000.0) * jnp.arange(quarter, dtype=F32) / quarter)
    ar = row[:, None] * freqs
    ac = col[:, None] * freqs
    return jnp.concatenate([jnp.sin(ar), jnp.cos(ar), jnp.sin(ac), jnp.cos(ac)], axis=-1)


def _dft_mats(n, scale):
    idx = jnp.arange(n, dtype=I32)
    ang = ((idx[:, None] * idx[None, :]) % n).astype(F32) * (2.0 * math.pi / n)
    return jnp.cos(ang) * scale, jnp.sin(ang) * scale


def _split_cols(w, sizes):
    out, start = [], 0
    for sz in sizes:
        out.append(w[..., start:start + sz])
        start += sz
    return out


def _layer_weights(d, w_in_l, gla_a_w_l, gla_a_b_l, dt_bias_l, a_log_l, ssd_d_l):
    qk = GLA_HEADS * GLA_DK
    vv = GLA_HEADS * GLA_DV
    inner = SSD_GROUPS * HEADS_PER_GROUP * SSD_HEADDIM
    gn = SSD_GROUPS * SSD_STATE
    nh = SSD_GROUPS * HEADS_PER_GROUP
    sizes = (qk, qk, vv, vv, GLA_RANK, GLA_RANK, d, inner, inner + 2 * gn, nh, nh, d, d, d)
    (wq, wk, wv, wr, wlf, wlb, wu, wz, wxbc, wdf, wdb, wga, wgb, wgc) = _split_cols(w_in_l, sizes)
    w_main = jnp.concatenate([wq, wk, wv, wr, wu, wz, wxbc, wga, wgb, wgc], axis=1).astype(MXU_DTYPE)
    hp = HEADS_PER_GROUP
    pad = lambda w, width: jnp.pad(w, ((0, 0), (0, width - w.shape[1])))
    small = [pad(jnp.concatenate([wlf, wlb], axis=1), LANES)]
    for g in range(SSD_GROUPS):
        small.append(pad(jnp.concatenate([wdf[:, g * hp:(g + 1) * hp], wdb[:, g * hp:(g + 1) * hp]], axis=1), LANES))
    w_small = jnp.concatenate(small, axis=1).astype(MXU_DTYPE)
    aw_pad = jnp.zeros((2, LANES, qk), F32)
    aw_pad = aw_pad.at[0, :GLA_RANK].set(gla_a_w_l[0]).at[1, GLA_RANK:2 * GLA_RANK].set(gla_a_w_l[1])
    aw_pad = aw_pad.astype(MXU_DTYPE)
    ab = gla_a_b_l.reshape(2, 1, qk)
    grp = lambda v: v.reshape(SSD_GROUPS, hp)
    brow = jnp.concatenate([grp(dt_bias_l[0]), grp(dt_bias_l[1]), jnp.zeros((SSD_GROUPS, LANES - 2 * hp), F32)], axis=1)
    arow = jnp.concatenate([grp(a_log_l[0]), grp(a_log_l[1]), jnp.zeros((SSD_GROUPS, LANES - 2 * hp), F32)], axis=1)
    d_exp = jnp.repeat(ssd_d_l, SSD_HEADDIM).reshape(SSD_GROUPS, 1, hp * SSD_HEADDIM)
    return w_main, w_small, aw_pad, ab, brow.reshape(SSD_GROUPS, 1, LANES), arow.reshape(SSD_GROUPS, 1, LANES), d_exp


def kernel(x, c, ctx, c_ctx, ada_w, ada_b, norm1_g, norm2_g, w_in, gla_a_w, gla_a_b, gla_norm_g, gla_proj,
           fnet_proj, ssd_conv_w, ssd_conv_b, ssd_dt_bias, ssd_a_log, ssd_d, ssd_norm_g, ssd_proj, w_out,
           router_w, router_b, exp_w1, exp_w3, exp_w2, final_g):
    bsz, t_lat, d = x.shape
    n_ctx = ctx.shape[1]
    depth = ada_w.shape[0]
    s = n_ctx + t_lat
    n = bsz * s
    nct = n_ctx // TM
    ns = d // LANES

    xs0 = jnp.concatenate([ctx, x + _grid_sincos(t_lat, d)], axis=1)

    rows = 16
    c_all = jnp.concatenate([c, c_ctx[None], jnp.zeros((rows - bsz - 1, d), F32)], axis=0)
    mods = _ada(c_all, ada_w, ada_b)
    mods = mods.reshape(depth, rows, N_MOD, d)
    mods = jnp.pad(mods, ((0, 0), (0, 0), (0, SUB - N_MOD), (0, 0)))
    modsel = jnp.stack([jnp.broadcast_to(mods[:, bsz:bsz + 1], (depth, bsz, SUB, d)), mods[:, :bsz]], axis=2)

    width = d
    gd = width // FNET_GROUPS
    cc, cs_ = _dft_mats(gd, gd ** -0.5)
    w_cs = jnp.concatenate([cc, cs_], axis=1).astype(MXU_DTYPE)
    cos_l, sin_l = (m.astype(MXU_DTYPE) for m in _dft_mats(t_lat, t_lat ** -0.5))
    cos_c, sin_c = (m.astype(MXU_DTYPE) for m in _dft_mats(n_ctx, n_ctx ** -0.5))

    rw_t = router_w.T
    rb = router_b.reshape(N_EXPERTS, 1)
    lo_tab = jnp.array([4 * (bk // N_PAIRS) + PAIR_LO[bk % N_PAIRS] for bk in range(N_BUCKETS)], I32)
    hi_tab = jnp.array([4 * (bk // N_PAIRS) + PAIR_HI[bk % N_PAIRS] for bk in range(N_BUCKETS)], I32)
    cap_rows = n + N_BUCKETS * TMX
    ntile = cap_rows // TMX

    xcur = xs0
    h = _normmod(xcur, norm1_g[0], modsel[0], nct)
    out = None
    for l in range(depth):
        w_main, w_small, aw_pad, ab, brow, arow, d_exp = _layer_weights(
            d, w_in[l], gla_a_w[l], gla_a_b[l], ssd_dt_bias[l], ssd_a_log[l], ssd_d[l])
        p, ps = _inproj(h, w_main, w_small)
        xbc = _conv(p, ssd_conv_w[l], ssd_conv_b[l], n_ctx)
        ga = _gla(p, ps, aw_pad, ab, gla_norm_g[l], n_ctx)
        yz = _ssd(p, xbc, ps, brow, arow, d_exp, n_ctx)
        v = _chan_dft(p, w_cs, width)
        fm = _pos_dft(v, cos_c, sin_c, cos_l, sin_l, width)
        xmid, ht, ri, rf, wcol, cnt = _merge(
            ga, fm, yz, p, xcur, modsel[l], norm2_g[l].reshape(1, d), ssd_norm_g[l].reshape(1, -1),
            gla_proj[l].astype(MXU_DTYPE), fnet_proj[l].astype(MXU_DTYPE), ssd_proj[l].astype(MXU_DTYPE),
            w_out[l].astype(MXU_DTYPE), rw_t, rb, nct)

        counts = cnt[:N_BUCKETS, 0]
        tiles_per = (counts + TMX - 1) // TMX
        tile_end = jnp.cumsum(tiles_per)
        tile_start = tile_end - tiles_per
        bucket = ri[0]
        pos = (jnp.take(tile_start, bucket) * TMX + ri[1]).reshape(1, n)
        tile_bucket = jnp.minimum(jnp.searchsorted(tile_end, jnp.arange(ntile, dtype=I32), side="right"),
                                  N_BUCKETS - 1).astype(I32)
        tile_a = jnp.take(lo_tab, tile_bucket)
        tile_b = jnp.take(hi_tab, tile_bucket)
        nvalid = tile_end[-1:].astype(I32)

        xs_sorted = _dispatch(pos, ht, jnp.zeros((cap_rows * ns, LANES), F32), ns)
        w13 = jnp.concatenate([exp_w1[l], exp_w3[l]], axis=2).astype(MXU_DTYPE)
        ys = _experts(tile_a, tile_b, nvalid, xs_sorted, w13, exp_w2[l].astype(MXU_DTYPE), d)
        final = l == depth - 1
        if final:
            xcur, out = _combine(pos, ys, xmid, wcol, modsel[l], modsel[l], final_g, nct, True)
        else:
            xcur, h = _combine(pos, ys, xmid, wcol, modsel[l], modsel[l + 1], norm1_g[l + 1], nct, False)
    return out
```

```python
import functools
import math

import jax
import jax.numpy as jnp
from jax import lax
from jax.experimental import pallas as pl
from jax.experimental.pallas import tpu as pltpu

F32 = jnp.float32
BF16 = jnp.bfloat16
I32 = jnp.int32
MXU_DTYPE = BF16

EPS = 1e-6
LOG2E = 1.4426950408889634
GRID_W = 64
N_MOD = 6

GLA_HEADS = 4
GLA_DK = 128
GLA_DV = 256
GLA_RANK = 16
GLA_TAU = 16.0
GLA_CHUNK = 64
GLA_HPS = 2
GLA_BLOCK = 256

FNET_GROUPS = 4

SSD_HEADDIM = 64
SSD_GROUPS = 4
SSD_STATE = 128
SSD_CONV = 5
CONV_EDGE = 16
SSD_CHUNK = 128
HEADS_PER_GROUP = 8

N_EXPERTS = 16
N_EXPERT_GROUPS = 4
EXPERTS_PER_GROUP = 4
D_FF = 512
N_PAIRS = 6
N_BUCKETS = N_EXPERT_GROUPS * N_PAIRS
PAIR_LO = (0, 0, 0, 1, 1, 2)
PAIR_HI = (1, 2, 3, 2, 3, 3)

TM = 256
TMX = 256
LANES = 128
SUB = 8
VMEM_LIMIT = 56 * 1024 * 1024

COL_Q, COL_K, COL_V, COL_R, COL_U, COL_Z, COL_X, COL_B, COL_C, COL_GA, COL_GB, COL_GC = (
    0, 512, 1024, 2048, 3072, 4096, 6144, 8192, 8704, 9216, 10240, 11264)
P_MAIN = 12288
P_SMALL = LANES * (1 + SSD_GROUPS)
IN_TN = 512


def _cparams(sem, vmem=VMEM_LIMIT):
    return pltpu.CompilerParams(dimension_semantics=sem, vmem_limit_bytes=vmem)


def _mm(a, b):
    return jnp.dot(a.astype(MXU_DTYPE), b.astype(MXU_DTYPE), preferred_element_type=F32)


def _mm_nt(a, b):
    return lax.dot_general(a.astype(MXU_DTYPE), b.astype(MXU_DTYPE), (((1,), (1,)), ((), ())),
                           preferred_element_type=F32)


def _mm_tn(a, b):
    return lax.dot_general(a.astype(MXU_DTYPE), b.astype(MXU_DTYPE), (((0,), (0,)), ((), ())),
                           preferred_element_type=F32)


def _split3(x):
    hi = x.astype(MXU_DTYPE)
    r1 = x - hi.astype(F32)
    mid = r1.astype(MXU_DTYPE)
    lo = (r1 - mid.astype(F32)).astype(MXU_DTYPE)
    return hi, mid, lo


def _mm_exact_lhs(m01, x):
    hi, mid, lo = _split3(x)
    m = m01.astype(MXU_DTYPE)
    return (jnp.dot(m, hi, preferred_element_type=F32) + jnp.dot(m, mid, preferred_element_type=F32)
            + jnp.dot(m, lo, preferred_element_type=F32))


def _sigmoid(x):
    return 1.0 / (1.0 + jnp.exp(-x))


def _silu(x):
    return x * _sigmoid(x)


def _softplus(x):
    return jnp.maximum(x, 0.0) + jnp.log(1.0 + jnp.exp(-jnp.abs(x)))


def _rms(x, g):
    return x * lax.rsqrt(jnp.mean(x * x, axis=-1, keepdims=True) + EPS) * g


def _modulate(x, g, shift, scale):
    return _rms(x, g) * (1.0 + scale) + shift


def _ada_kernel(c_ref, w_ref, b_ref, o_ref):
    o_ref[0] = _mm(_silu(c_ref[...]), w_ref[0]) + b_ref[0]


def _ada(c_all, ada_w, ada_b):
    depth, d, nd = ada_w.shape
    rows = c_all.shape[0]
    tn = 1024
    return pl.pallas_call(
        _ada_kernel,
        grid=(depth, nd // tn),
        in_specs=[pl.BlockSpec((rows, d), lambda l, j: (0, 0)),
                  pl.BlockSpec((1, d, tn), lambda l, j: (l, 0, j)),
                  pl.BlockSpec((1, 1, tn), lambda l, j: (l, 0, j))],
        out_specs=pl.BlockSpec((1, rows, tn), lambda l, j: (l, 0, j)),
        out_shape=jax.ShapeDtypeStruct((depth, rows, nd), F32),
        compiler_params=_cparams(("arbitrary", "arbitrary")),
        name="ada_mod",
    )(c_all, ada_w, ada_b.reshape(depth, 1, nd))


def _normmod_kernel(x_ref, g_ref, mod_ref, h_ref):
    mod = mod_ref[0, 0]
    h_ref[0] = _modulate(x_ref[0], g_ref[...], mod[0:1], mod[1:2]).astype(h_ref.dtype)


def _normmod(x, g, modsel, n_ctx_tiles):
    b, s, d = x.shape
    return pl.pallas_call(
        _normmod_kernel,
        grid=(b, s // TM),
        in_specs=[pl.BlockSpec((1, TM, d), lambda i, j: (i, j, 0)),
                  pl.BlockSpec((1, d), lambda i, j: (0, 0)),
                  pl.BlockSpec((1, 1, SUB, d), lambda i, j: (i, jnp.where(j < n_ctx_tiles, 0, 1), 0, 0))],
        out_specs=pl.BlockSpec((1, TM, d), lambda i, j: (i, j, 0)),
        out_shape=jax.ShapeDtypeStruct((b, s, d), MXU_DTYPE),
        compiler_params=_cparams(("arbitrary", "arbitrary")),
        name="norm_mod",
    )(x, g.reshape(1, d), modsel)


def _inproj_kernel(h_ref, w_ref, ws_ref, p_ref, ps_ref):
    h = h_ref[0]
    p_ref[0] = _mm(h, w_ref[...]).astype(p_ref.dtype)

    @pl.when(pl.program_id(1) == 0)
    def _():
        ps_ref[0] = _mm(h, ws_ref[...])


def _inproj(h, w_main, w_small):
    b, s, d = h.shape
    return pl.pallas_call(
        _inproj_kernel,
        grid=(b, P_MAIN // IN_TN),
        in_specs=[pl.BlockSpec((1, s, d), lambda i, j: (i, 0, 0)),
                  pl.BlockSpec((d, IN_TN), lambda i, j: (0, j)),
                  pl.BlockSpec((d, P_SMALL), lambda i, j: (0, 0))],
        out_specs=[pl.BlockSpec((1, s, IN_TN), lambda i, j: (i, 0, j)),
                   pl.BlockSpec((1, s, P_SMALL), lambda i, j: (i, 0, 0))],
        out_shape=[jax.ShapeDtypeStruct((b, s, P_MAIN), MXU_DTYPE),
                   jax.ShapeDtypeStruct((b, s, P_SMALL), F32)],
        compiler_params=_cparams(("arbitrary", "arbitrary")),
        name="in_proj",
    )(h, w_main, w_small)


def _conv_kernel(p_ref, w_ref, b_ref, o_ref, *, n_ctx):
    x = p_ref[0].astype(F32)
    s = x.shape[0]
    w = w_ref[...]
    bias = b_ref[...]
    half = SSD_CONV // 2

    def taps(xw, row0, masked):
        r = xw.shape[0]
        acc = jnp.zeros_like(xw) + bias
        if masked:
            row = row0 + lax.broadcasted_iota(I32, (r, 1), 0)
            seg_lo = jnp.where(row < n_ctx, 0, n_ctx)
            seg_hi = jnp.where(row < n_ctx, n_ctx, s)
        for k in range(SSD_CONV):
            dlt = k - half
            xs = xw if dlt == 0 else pltpu.roll(xw, shift=(-dlt) % r, axis=0)
            if masked:
                t2 = row + dlt
                ok = jnp.broadcast_to(jnp.where((t2 >= seg_lo) & (t2 < seg_hi), 1.0, 0.0), xw.shape)
                xs = jnp.where(ok > 0.5, xs, 0.0)
            acc = acc + w[k:k + 1] * xs
        return _silu(acc)

    o_ref[0] = taps(x, 0, False).astype(o_ref.dtype)
    eb, win = CONV_EDGE, 2 * CONV_EDGE
    for r0 in (0, n_ctx - eb, n_ctx, s - eb):
        a = min(max(r0 - SUB, 0), s - win)
        o_ref[0, r0:r0 + eb, :] = taps(x[a:a + win], a, True)[r0 - a:r0 - a + eb].astype(o_ref.dtype)


def _conv(p, conv_w, conv_b, n_ctx):
    b, s, _ = p.shape
    c = conv_w.shape[1]
    tc = 256
    return pl.pallas_call(
        functools.partial(_conv_kernel, n_ctx=n_ctx),
        grid=(b, c // tc),
        in_specs=[pl.BlockSpec((1, s, tc), lambda i, j: (i, 0, COL_X // tc + j)),
                  pl.BlockSpec((SSD_CONV, tc), lambda i, j: (0, j)),
                  pl.BlockSpec((1, tc), lambda i, j: (0, j))],
        out_specs=pl.BlockSpec((1, s, tc), lambda i, j: (i, 0, j)),
        out_shape=jax.ShapeDtypeStruct((b, s, c), MXU_DTYPE),
        compiler_params=_cparams(("arbitrary", "arbitrary")),
        name="ssd_conv",
    )(p, conv_w, conv_b.reshape(1, c))


def _gla_kernel(q_ref, k_ref, v_ref, r_ref, ps_ref, aw_ref, ab_ref, g_ref, o_ref,
                la_s, od_s, st_s, *, n_ctx):
    L = GLA_CHUNK
    R = GLA_BLOCK
    cpb = R // L
    dk, dv = GLA_DK, GLA_DV
    s = q_ref.shape[1]
    nb = s // R
    nbc = n_ctx // R
    lr = ps_ref[0]
    for d in range(2):
        pre = _mm(lr, aw_ref[d]) + ab_ref[d]
        la_s[d] = -_softplus(-pre) * (1.0 / GLA_TAU)

    ri = lax.broadcasted_iota(I32, (R, R), 0)
    ci = lax.broadcasted_iota(I32, (R, R), 1)
    same = (ri // L) == (ci // L)
    scale = dk ** -0.5
    st_s[...] = jnp.zeros_like(st_s)

    def per_chunk(x, idx):
        x3 = x.reshape(cpb, L, x.shape[-1])
        return jnp.broadcast_to(x3[:, idx:idx + 1, :], x3.shape).reshape(x.shape)

    def block(off, hh, reverse, st, keep, bcum_all):
        q = q_ref[0, pl.ds(off, R), hh * dk:(hh + 1) * dk].astype(F32) * scale
        k = k_ref[0, pl.ds(off, R), hh * dk:(hh + 1) * dk].astype(F32)
        v = v_ref[0, pl.ds(off, R), hh * dv:(hh + 1) * dv]
        bcum = bcum_all[:, hh * dk:(hh + 1) * dk]
        b_mid = per_chunk(bcum, L // 2 if reverse else L // 2 - 1)
        b_last = per_chunk(bcum, 0 if reverse else L - 1)
        sc = _mm_nt(q * jnp.exp(bcum - b_mid), k * jnp.exp(b_mid - bcum))
        o_intra = _mm(jnp.where(keep, sc, 0.0), v)
        qb = (q * jnp.exp(bcum)).astype(MXU_DTYPE)
        kd = (k * jnp.exp(b_last - bcum)).astype(MXU_DTYPE)
        dec = jnp.exp(b_last)
        o_inter = [None] * cpb
        for c in (range(cpb - 1, -1, -1) if reverse else range(cpb)):
            rows = slice(c * L, (c + 1) * L)
            o_inter[c] = _mm_nt(qb[rows], st)
            st = dec[c * L:c * L + 1] * st + _mm_tn(v[rows], kd[rows])
        return o_intra + jnp.concatenate(o_inter, axis=0), st

    @pl.loop(0, nb)
    def _(i):
        off_f = pl.multiple_of(i * R, R)
        jb = jnp.where(i < nbc, nbc - 1 - i, nb + nbc - 1 - i)
        off_b = pl.multiple_of(jb * R, R)
        offs = (off_f, off_b)
        keeps = (same & (ci <= ri), same & (ci >= ri))
        bcums = [_mm_exact_lhs(jnp.where(keeps[d], 1.0, 0.0), la_s[d, pl.ds(offs[d], R), :]) for d in range(2)]
        chains = [(d, hh) for hh in range(GLA_HPS) for d in range(2)]
        states = [st_s[d, hh] for d, hh in chains]
        res = [block(offs[d], hh, bool(d), st, keeps[d], bcums[d]) for (d, hh), st in zip(chains, states)]
        for (d, hh), (o, st) in zip(chains, res):
            od_s[d, pl.ds(off_b if d else off_f, R), hh * dv:(hh + 1) * dv] = o
            st_s[d, hh] = st

    g = g_ref[...]

    @pl.loop(0, s // L)
    def _(c):
        off = pl.multiple_of(c * L, L)
        o = od_s[0, pl.ds(off, L), :] + od_s[1, pl.ds(off, L), :]
        r = r_ref[0, pl.ds(off, L), :].astype(F32)
        outs = [_rms(o[:, hh * dv:(hh + 1) * dv], g) for hh in range(GLA_HPS)]
        o_ref[0, pl.ds(off, L), :] = (jnp.concatenate(outs, axis=1) * _silu(r)).astype(o_ref.dtype)


def _gla(p, ps, aw_pad, ab, norm_g, n_ctx):
    b, s, _ = p.shape
    dk, dv = GLA_HPS * GLA_DK, GLA_HPS * GLA_DV
    return pl.pallas_call(
        functools.partial(_gla_kernel, n_ctx=n_ctx),
        grid=(b, GLA_HEADS // GLA_HPS),
        in_specs=[pl.BlockSpec((1, s, dk), lambda i, h: (i, 0, COL_Q // dk + h)),
                  pl.BlockSpec((1, s, dk), lambda i, h: (i, 0, COL_K // dk + h)),
                  pl.BlockSpec((1, s, dv), lambda i, h: (i, 0, COL_V // dv + h)),
                  pl.BlockSpec((1, s, dv), lambda i, h: (i, 0, COL_R // dv + h)),
                  pl.BlockSpec((1, s, LANES), lambda i, h: (i, 0, 0)),
                  pl.BlockSpec((2, LANES, dk), lambda i, h: (0, 0, h)),
                  pl.BlockSpec((2, 1, dk), lambda i, h: (0, 0, h)),
                  pl.BlockSpec((1, GLA_DV), lambda i, h: (0, 0))],
        out_specs=pl.BlockSpec((1, s, dv), lambda i, h: (i, 0, h)),
        out_shape=jax.ShapeDtypeStruct((b, s, GLA_HEADS * GLA_DV), MXU_DTYPE),
        scratch_shapes=[pltpu.VMEM((2, s, dk), F32), pltpu.VMEM((2, s, dv), F32),
                        pltpu.VMEM((2, GLA_HPS, GLA_DV, GLA_DK), F32)],
        compiler_params=_cparams(("arbitrary", "arbitrary")),
        name="gla",
    )(p, p, p, p, ps, aw_pad, ab, norm_g.reshape(1, GLA_DV))


def _ssd_kernel(z_ref, x_ref, bm_ref, cm_ref, ps_ref, brow_ref, alog_ref, d_ref, o_ref,
                dt_s, acs_s, rt_s, cb_s, y_s, st_s, *, n_ctx):
    L = SSD_CHUNK
    s = x_ref.shape[1]
    n = s // L
    nc = n_ctx // L
    hp = HEADS_PER_GROUP
    npair = hp // 2
    neg = -1e30
    dt_s[...] = _softplus(ps_ref[0] + brow_ref[0])
    a_row = -jnp.exp(alog_ref[0])

    ri = lax.broadcasted_iota(I32, (L, L), 0)
    ci = lax.broadcasted_iota(I32, (L, L), 1)
    lane = lax.broadcasted_iota(I32, (1, LANES), 1)
    low = lane < SSD_HEADDIM
    keeps = (ci <= ri, ci >= ri)
    er = lax.broadcasted_iota(I32, (LANES, hp * SSD_HEADDIM), 0)
    ec = lax.broadcasted_iota(I32, (LANES, hp * SSD_HEADDIM), 1) // SSD_HEADDIM
    esel = [jnp.where(er == d * hp + ec, 1.0, 0.0).astype(MXU_DTYPE) for d in range(2)]

    @pl.loop(0, n, unroll=2)
    def _(c):
        off = pl.multiple_of(c * L, L)
        dtc = dt_s[pl.ds(off, L), :]
        wc = dtc * a_row
        pre = _mm_exact_lhs(jnp.where(keeps[0], 1.0, 0.0), wc)
        suf = pre[L - 1:L] - pre + wc
        acs = jnp.where(lane < hp, pre, suf)
        acs_s[pl.ds(off, L), :] = acs * LOG2E
        rt_s[pl.ds(off, L), :] = ((acs - jnp.log(dtc)) * LOG2E).T
        cb_s[pl.ds(off, L), :] = _mm_nt(cm_ref[0, pl.ds(off, L), :], bm_ref[0, pl.ds(off, L), :])

    st_s[...] = jnp.zeros_like(st_s)

    def chunk(off, d, st):
        base = d * hp
        keep = keeps[d]
        i_last = 0 if d else L - 1
        xb = x_ref[0, pl.ds(off, L), :]
        bm = bm_ref[0, pl.ds(off, L), :]
        cm = cm_ref[0, pl.ds(off, L), :]
        dtc = dt_s[pl.ds(off, L), :]
        acs = acs_s[pl.ds(off, L), :]
        rt = rt_s[pl.ds(off, L), :]
        cb = cb_s[pl.ds(off, L), :]
        mine = (lane >= base) & (lane < base + hp)
        to_end = dtc * jnp.exp2(jnp.where(mine, acs[i_last:i_last + 1] - acs, 0.0))
        hi = to_end.astype(MXU_DTYPE)
        lo = (to_end - hi.astype(F32)).astype(MXU_DTYPE)
        to_end_e = (jnp.dot(hi, esel[d], preferred_element_type=F32)
                    + jnp.dot(lo, esel[d], preferred_element_type=F32))
        y_off = _mm(cm, st)
        ys, decs = [], []
        for pr in range(npair):
            a0 = acs[:, base + 2 * pr:base + 2 * pr + 1]
            a1 = acs[:, base + 2 * pr + 1:base + 2 * pr + 2]
            m2 = jnp.concatenate(
                [cb * jnp.exp2(jnp.where(keep, a0 - rt[base + 2 * pr:base + 2 * pr + 1, :], neg)),
                 cb * jnp.exp2(jnp.where(keep, a1 - rt[base + 2 * pr + 1:base + 2 * pr + 2, :], neg))], axis=1)
            xp = xb[:, pr * LANES:(pr + 1) * LANES]
            zero = jnp.zeros_like(xp)
            bd = jnp.concatenate([jnp.where(low, xp, zero), jnp.where(low, zero, xp)], axis=0)
            dec = jnp.exp2(jnp.where(low, a0, a1))
            ys.append(_mm(m2, bd) + y_off[:, pr * LANES:(pr + 1) * LANES] * dec)
            decs.append(dec[i_last:i_last + 1])
        st = jnp.concatenate(decs, axis=1) * st + _mm_tn(bm, xb.astype(F32) * to_end_e)
        return jnp.concatenate(ys, axis=1), st

    @pl.loop(0, n)
    def _(i):
        jb = jnp.where(i < nc, nc - 1 - i, n + nc - 1 - i)
        offs = (pl.multiple_of(i * L, L), pl.multiple_of(jb * L, L))
        states = [st_s[d] for d in range(2)]
        res = [chunk(offs[d], d, states[d]) for d in range(2)]
        for d, (y, st) in enumerate(res):
            y_s[d, pl.ds(offs[d], L), :] = y
            st_s[d] = st

    dsk = d_ref[0]

    @pl.loop(0, n)
    def _(c):
        off = pl.multiple_of(c * L, L)
        y = y_s[0, pl.ds(off, L), :] + y_s[1, pl.ds(off, L), :] + dsk * x_ref[0, pl.ds(off, L), :].astype(F32)
        z = z_ref[0, pl.ds(off, L), :].astype(F32)
        o_ref[0, pl.ds(off, L), :] = (y * _silu(z)).astype(o_ref.dtype)


def _ssd(p, xbc, ps, brow, alog_row, d_exp, n_ctx):
    b, s, _ = p.shape
    gw = HEADS_PER_GROUP * SSD_HEADDIM
    inner = SSD_GROUPS * gw
    ns = SSD_STATE
    return pl.pallas_call(
        functools.partial(_ssd_kernel, n_ctx=n_ctx),
        grid=(b, SSD_GROUPS),
        in_specs=[pl.BlockSpec((1, s, gw), lambda i, g: (i, 0, COL_Z // gw + g)),
                  pl.BlockSpec((1, s, gw), lambda i, g: (i, 0, g)),
                  pl.BlockSpec((1, s, ns), lambda i, g: (i, 0, inner // ns + g)),
                  pl.BlockSpec((1, s, ns), lambda i, g: (i, 0, inner // ns + SSD_GROUPS + g)),
                  pl.BlockSpec((1, s, LANES), lambda i, g: (i, 0, 1 + g)),
                  pl.BlockSpec((1, 1, LANES), lambda i, g: (g, 0, 0)),
                  pl.BlockSpec((1, 1, LANES), lambda i, g: (g, 0, 0)),
                  pl.BlockSpec((1, 1, gw), lambda i, g: (g, 0, 0))],
        out_specs=pl.BlockSpec((1, s, gw), lambda i, g: (i, 0, g)),
        out_shape=jax.ShapeDtypeStruct((b, s, inner), MXU_DTYPE),
        scratch_shapes=[pltpu.VMEM((s, LANES), F32), pltpu.VMEM((s, LANES), F32),
                        pltpu.VMEM((s, LANES), F32), pltpu.VMEM((s, LANES), F32),
                        pltpu.VMEM((2, s, gw), F32), pltpu.VMEM((2, ns, gw), F32)],
        compiler_params=_cparams(("arbitrary", "arbitrary")),
        name="ssd",
    )(p, xbc, xbc, xbc, ps, brow, alog_row, d_exp)


def _chan_dft_kernel(u_ref, w_ref, o_ref):
    u = u_ref[0]
    gd = w_ref.shape[0]
    w = w_ref[...]
    cs, sn = [], []
    for g in range(FNET_GROUPS):
        r = _mm(u[:, g * gd:(g + 1) * gd], w)
        cs.append(r[:, :gd])
        sn.append(r[:, gd:])
    o_ref[0] = jnp.concatenate(cs + sn, axis=1).astype(o_ref.dtype)


def _chan_dft(p, w_cs, width):
    b, s, _ = p.shape
    gd = width // FNET_GROUPS
    return pl.pallas_call(
        _chan_dft_kernel,
        grid=(b, s // TM),
        in_specs=[pl.BlockSpec((1, TM, width), lambda i, j: (i, j, COL_U // width)),
                  pl.BlockSpec((gd, 2 * gd), lambda i, j: (0, 0))],
        out_specs=pl.BlockSpec((1, TM, 2 * width), lambda i, j: (i, j, 0)),
        out_shape=jax.ShapeDtypeStruct((b, s, 2 * width), MXU_DTYPE),
        compiler_params=_cparams(("arbitrary", "arbitrary")),
        name="fnet_chan",
    )(p, w_cs)


def _pos_dft_kernel(cc_ref, sc_ref, cl_ref, sl_ref, uc_ref, us_ref, o_ref, *, n_ctx):
    j = pl.program_id(1)
    nct = n_ctx // TM

    @pl.when(j < nct)
    def _():
        o_ref[0] = (_mm(cc_ref[...], uc_ref[0, :n_ctx, :]) - _mm(sc_ref[...], us_ref[0, :n_ctx, :])).astype(o_ref.dtype)

    @pl.when(j >= nct)
    def _():
        o_ref[0] = (_mm(cl_ref[...], uc_ref[0, n_ctx:, :]) - _mm(sl_ref[...], us_ref[0, n_ctx:, :])).astype(o_ref.dtype)


def _pos_dft(v, cos_c, sin_c, cos_l, sin_l, width):
    b, s, _ = v.shape
    n_ctx = cos_c.shape[0]
    t = cos_l.shape[0]
    nct = n_ctx // TM
    cidx = lambda i, j: (jnp.minimum(j, nct - 1), 0)
    lidx = lambda i, j: (jnp.maximum(j - nct, 0), 0)
    return pl.pallas_call(
        functools.partial(_pos_dft_kernel, n_ctx=n_ctx),
        grid=(b, s // TM),
        in_specs=[pl.BlockSpec((TM, n_ctx), cidx),
                  pl.BlockSpec((TM, n_ctx), cidx),
                  pl.BlockSpec((TM, t), lidx),
                  pl.BlockSpec((TM, t), lidx),
                  pl.BlockSpec((1, s, width), lambda i, j: (i, 0, 0)),
                  pl.BlockSpec((1, s, width), lambda i, j: (i, 0, 1))],
        out_specs=pl.BlockSpec((1, TM, width), lambda i, j: (i, j, 0)),
        out_shape=jax.ShapeDtypeStruct((b, s, width), MXU_DTYPE),
        compiler_params=_cparams(("arbitrary", "arbitrary")),
        name="fnet_pos",
    )(cos_c, sin_c, cos_l, sin_l, v, v)


def _merge_kernel(ga_ref, fm_ref, yz_ref, g1_ref, g2_ref, g3_ref, x_ref, mod_ref, n2g_ref, sg_ref,
                  wa_ref, wb_ref, wc_ref, wo_ref, rw_ref, rb_ref,
                  xo_ref, ht_ref, ri_ref, rf_ref, wcol_ref, cnt_ref, carry_s):
    first = (pl.program_id(0) == 0) & (pl.program_id(1) == 0)

    @pl.when(first)
    def _():
        carry_s[...] = jnp.zeros_like(carry_s)

    mod = mod_ref[0, 0]
    ya = _mm(ga_ref[0], wa_ref[...])
    yb = _mm(fm_ref[0], wb_ref[...])
    yz = yz_ref[0].astype(F32)
    yc = _mm(_rms(yz, sg_ref[...]), wc_ref[...])
    y = (_sigmoid(g1_ref[0].astype(F32)) * ya + _sigmoid(g2_ref[0].astype(F32)) * yb
         + _sigmoid(g3_ref[0].astype(F32)) * yc)
    xn = x_ref[0] + mod[2:3] * _mm(y, wo_ref[...])
    xo_ref[0] = xn
    h2 = _modulate(xn, n2g_ref[...], mod[3:4], mod[4:5])
    d = h2.shape[1]
    tm = h2.shape[0]
    for sidx in range(d // LANES):
        ht_ref[pl.ds(sidx, tm, stride=d // LANES), :] = h2[:, sidx * LANES:(sidx + 1) * LANES]

    hh, hm, _hl = _split3(h2)
    rh, rm, _rl = _split3(rw_ref[...])
    nt = (((1,), (1,)), ((), ()))
    lg = (lax.dot_general(rh, hh, nt, preferred_element_type=F32)
          + lax.dot_general(rh, hm, nt, preferred_element_type=F32)
          + lax.dot_general(rm, hh, nt, preferred_element_type=F32))
    sc = _sigmoid(lg)
    sel = sc + rb_ref[...]
    srow = [sel[e:e + 1] for e in range(N_EXPERTS)]
    urow = [sc[e:e + 1] for e in range(N_EXPERTS)]
    epg = EXPERTS_PER_GROUP

    best_v = best_pair = best_g = best_slo = best_shi = None
    for g in range(N_EXPERT_GROUPS):
        a = srow[g * epg:(g + 1) * epg]
        u = urow[g * epg:(g + 1) * epg]
        gs = None
        for i in range(epg):
            for j in range(i + 1, epg):
                ps_ = a[i] + a[j]
                gs = ps_ if gs is None else jnp.maximum(gs, ps_)
        m1, i1 = a[0], jnp.zeros_like(a[0], dtype=I32)
        for i in range(1, epg):
            up = a[i] > m1
            i1 = jnp.where(up, i, i1)
            m1 = jnp.where(up, a[i], m1)
        m2 = jnp.full_like(a[0], -jnp.inf)
        i2 = jnp.full_like(i1, -1)
        for i in range(epg):
            cand = (i1 != i) & (a[i] > m2)
            i2 = jnp.where(cand, i, i2)
            m2 = jnp.where(cand, a[i], m2)
        lo = jnp.minimum(i1, i2)
        hi = jnp.maximum(i1, i2)
        pair = jnp.where(lo == 0, 0, jnp.where(lo == 1, 3, 5)) + (hi - lo - 1)
        s_lo = jnp.where(lo == 0, u[0], jnp.where(lo == 1, u[1], u[2]))
        s_hi = jnp.where(hi == 1, u[1], jnp.where(hi == 2, u[2], u[3]))
        if g == 0:
            best_v, best_pair, best_g, best_slo, best_shi = gs, pair, jnp.zeros_like(pair), s_lo, s_hi
        else:
            up = gs > best_v
            best_v = jnp.where(up, gs, best_v)
            best_pair = jnp.where(up, pair, best_pair)
            best_g = jnp.where(up, g, best_g)
            best_slo = jnp.where(up, s_lo, best_slo)
            best_shi = jnp.where(up, s_hi, best_shi)
    bucket = best_g * N_PAIRS + best_pair
    tot = best_slo + best_shi
    w_lo = best_slo / tot
    w_hi = best_shi / tot

    nb = carry_s.shape[0]
    oh = (lax.broadcasted_iota(I32, (nb, tm), 0) == bucket).astype(F32)
    tri = (lax.broadcasted_iota(I32, (tm, tm), 0) <= lax.broadcasted_iota(I32, (tm, tm), 1))
    csum = _mm(oh, jnp.where(tri, 1.0, 0.0))
    carry = carry_s[...]
    rank = jnp.sum(oh * (csum - 1.0 + carry), axis=0, keepdims=True)
    carry = carry + jnp.sum(oh, axis=1, keepdims=True)
    carry_s[...] = carry
    cnt_ref[...] = jnp.broadcast_to(carry, cnt_ref.shape).astype(I32)

    zi = jnp.zeros((SUB - 2, tm), I32)
    ri_ref[...] = jnp.concatenate([bucket, rank.astype(I32), zi], axis=0)
    wrows = jnp.concatenate([w_lo, w_hi, jnp.zeros((SUB - 2, tm), F32)], axis=0)
    rf_ref[...] = wrows
    eye = (lax.broadcasted_iota(I32, (SUB, LANES), 0) == lax.broadcasted_iota(I32, (SUB, LANES), 1))
    eye = jnp.where(eye, 1.0, 0.0).astype(MXU_DTYPE)
    tn = (((0,), (0,)), ((), ()))
    p1, p2, p3 = _split3(wrows)
    wcol_ref[...] = (lax.dot_general(p1, eye, tn, preferred_element_type=F32)
                     + lax.dot_general(p2, eye, tn, preferred_element_type=F32)
                     + lax.dot_general(p3, eye, tn, preferred_element_type=F32))


def _merge(ga, fm, yz, p, x, modsel, n2g, ssd_g, w_a, w_b, w_c, w_o, rw_t, rb, n_ctx_tiles):
    b, s, d = x.shape
    nt = s // TM
    n = b * s
    di = yz.shape[2]
    nbp = 32
    tok = lambda i, j: (i, j, 0)
    flat = lambda i, j: (0, i * nt + j)
    const = lambda i, j: (0, 0)
    return pl.pallas_call(
        _merge_kernel,
        grid=(b, nt),
        in_specs=[pl.BlockSpec((1, TM, d), tok),
                  pl.BlockSpec((1, TM, d), tok),
                  pl.BlockSpec((1, TM, di), tok),
                  pl.BlockSpec((1, TM, d), lambda i, j: (i, j, COL_GA // d)),
                  pl.BlockSpec((1, TM, d), lambda i, j: (i, j, COL_GB // d)),
                  pl.BlockSpec((1, TM, d), lambda i, j: (i, j, COL_GC // d)),
                  pl.BlockSpec((1, TM, d), tok),
                  pl.BlockSpec((1, 1, SUB, d), lambda i, j: (i, jnp.where(j < n_ctx_tiles, 0, 1), 0, 0)),
                  pl.BlockSpec((1, d), const),
                  pl.BlockSpec((1, di), const),
                  pl.BlockSpec((d, d), const),
                  pl.BlockSpec((d, d), const),
                  pl.BlockSpec((di, d), const),
                  pl.BlockSpec((d, d), const),
                  pl.BlockSpec((N_EXPERTS, d), const),
                  pl.BlockSpec((N_EXPERTS, 1), const)],
        out_specs=[pl.BlockSpec((1, TM, d), tok),
                   pl.BlockSpec((TM * (d // LANES), LANES), lambda i, j: (i * nt + j, 0)),
                   pl.BlockSpec((SUB, TM), flat),
                   pl.BlockSpec((SUB, TM), flat),
                   pl.BlockSpec((TM, LANES), lambda i, j: (i * nt + j, 0)),
                   pl.BlockSpec((nbp, LANES), const)],
        out_shape=[jax.ShapeDtypeStruct((b, s, d), F32),
                   jax.ShapeDtypeStruct((n * (d // LANES), LANES), F32),
                   jax.ShapeDtypeStruct((SUB, n), I32),
                   jax.ShapeDtypeStruct((SUB, n), F32),
                   jax.ShapeDtypeStruct((n, LANES), F32),
                   jax.ShapeDtypeStruct((nbp, LANES), I32)],
        scratch_shapes=[pltpu.VMEM((nbp, 1), F32)],
        compiler_params=_cparams(("arbitrary", "arbitrary")),
        name="merge_route",
    )(ga, fm, yz, p, p, p, x, modsel, n2g, ssd_g, w_a, w_b, w_c, w_o, rw_t, rb)


def _row_copy(src, dst, src_row, dst_row, rows, sem):
    def at(ref, r):
        start = r * rows if isinstance(r, int) else pl.multiple_of(r * rows, rows)
        return ref.at[pl.ds(start, rows)]
    return pltpu.make_async_copy(at(src, src_row), at(dst, dst_row), sem)


def _dispatch_kernel(pos_ref, ht_ref, xs_in_ref, xs_ref, sem, *, rows):
    del xs_in_ref
    copies = [_row_copy(ht_ref, xs_ref, t, pos_ref[0, t], rows, sem) for t in range(TM)]
    for cp in copies:
        cp.start()
    for cp in copies:
        cp.wait()


def _dispatch(pos, ht, xs_init, rows):
    n = pos.shape[1]
    return pl.pallas_call(
        functools.partial(_dispatch_kernel, rows=rows),
        grid=(n // TM,),
        in_specs=[pl.BlockSpec((1, TM), lambda i: (0, i), memory_space=pltpu.SMEM),
                  pl.BlockSpec((TM * rows, LANES), lambda i: (i, 0)),
                  pl.BlockSpec(memory_space=pl.ANY)],
        out_specs=pl.BlockSpec(memory_space=pl.ANY),
        out_shape=jax.ShapeDtypeStruct(xs_init.shape, xs_init.dtype),
        input_output_aliases={2: 0},
        scratch_shapes=[pltpu.SemaphoreType.DMA(())],
        compiler_params=_cparams(("arbitrary",)),
        name="moe_dispatch",
    )(pos, ht, xs_init)


def _expert_kernel(ta_ref, tb_ref, nv_ref, xs_ref, w13a_ref, w2a_ref, w13b_ref, w2b_ref, ys_ref):
    del ta_ref, tb_ref

    @pl.when(pl.program_id(0) >= nv_ref[0])
    def _():
        ys_ref[...] = jnp.zeros_like(ys_ref)

    @pl.when(pl.program_id(0) < nv_ref[0])
    def _():
        ns = xs_ref.shape[0] // TMX
        x = jnp.concatenate([xs_ref[pl.ds(sidx, TMX, stride=ns), :] for sidx in range(ns)], axis=1)
        x = x.astype(MXU_DTYPE)
        for kk, (w13_ref, w2_ref) in enumerate(((w13a_ref, w2a_ref), (w13b_ref, w2b_ref))):
            h13 = _mm(x, w13_ref[0])
            he = _silu(h13[:, :D_FF]) * h13[:, D_FF:]
            y = _mm(he, w2_ref[0])
            for sidx in range(ns):
                ys_ref[pl.ds(kk * ns + sidx, TMX, stride=2 * ns), :] = y[:, sidx * LANES:(sidx + 1) * LANES]


def _experts(tile_a, tile_b, nvalid, xs, w13, w2, d):
    ns = d // LANES
    ntile = xs.shape[0] // (TMX * ns)
    last = lambda i, nv: jnp.minimum(i, nv[0] - 1)
    grid_spec = pltpu.PrefetchScalarGridSpec(
        num_scalar_prefetch=3,
        grid=(ntile,),
        in_specs=[pl.BlockSpec((TMX * ns, LANES), lambda i, ta, tb, nv: (last(i, nv), 0)),
                  pl.BlockSpec((1, d, 2 * D_FF), lambda i, ta, tb, nv: (ta[last(i, nv)], 0, 0)),
                  pl.BlockSpec((1, D_FF, d), lambda i, ta, tb, nv: (ta[last(i, nv)], 0, 0)),
                  pl.BlockSpec((1, d, 2 * D_FF), lambda i, ta, tb, nv: (tb[last(i, nv)], 0, 0)),
                  pl.BlockSpec((1, D_FF, d), lambda i, ta, tb, nv: (tb[last(i, nv)], 0, 0))],
        out_specs=pl.BlockSpec((TMX * 2 * ns, LANES), lambda i, ta, tb, nv: (i, 0)),
    )
    return pl.pallas_call(
        _expert_kernel,
        grid_spec=grid_spec,
        out_shape=jax.ShapeDtypeStruct((ntile * TMX * 2 * ns, LANES), F32),
        compiler_params=_cparams(("arbitrary",)),
        name="moe_experts",
    )(tile_a, tile_b, nvalid, xs, w13, w2, w13, w2)


def _combine_kernel(pos_ref, posn_ref, ys_ref, x_ref, wcol_ref, mod_ref, modn_ref, g_ref, xo_ref, ho_ref, buf, sem,
                    *, rows, final):
    step = pl.program_id(0) * pl.num_programs(1) + pl.program_id(1)
    last = pl.num_programs(0) * pl.num_programs(1) - 1
    slot = step % 2

    def gathers(p_ref, sl):
        return [_row_copy(ys_ref, buf.at[sl], p_ref[0, t], t, rows, sem.at[sl]) for t in range(TM)]

    @pl.when(step == 0)
    def _():
        for cp in gathers(pos_ref, 0):
            cp.start()

    @pl.when(step < last)
    def _():
        for cp in gathers(posn_ref, 1 - slot):
            cp.start()

    for cp in gathers(pos_ref, slot):
        cp.wait()

    ns = rows // 2
    ya = jnp.concatenate([buf[slot, pl.ds(sidx, TM, stride=rows), :] for sidx in range(ns)], axis=1)
    yb = jnp.concatenate([buf[slot, pl.ds(ns + sidx, TM, stride=rows), :] for sidx in range(ns)], axis=1)
    wc = wcol_ref[...]
    y = wc[:, 0:1] * ya + wc[:, 1:2] * yb
    mod = mod_ref[0, 0]
    xn = x_ref[0] + mod[5:6] * y
    xo_ref[0] = xn
    if final:
        ho_ref[0] = _rms(xn, g_ref[...]).astype(ho_ref.dtype)
    else:
        modn = modn_ref[0, 0]
        ho_ref[0] = _modulate(xn, g_ref[...], modn[0:1], modn[1:2]).astype(ho_ref.dtype)


def _combine(pos, ys, x, wcol, modsel, modsel_next, g_next, n_ctx_tiles, final):
    b, s, d = x.shape
    nt = s // TM
    rows = 2 * (d // LANES)
    tok = lambda i, j: (i, j, 0)
    msel = lambda i, j: (i, jnp.where(j < n_ctx_tiles, 0, 1), 0, 0)
    if final:
        t_lat = s - n_ctx_tiles * TM
        h_shape = jax.ShapeDtypeStruct((b, t_lat, d), F32)
        h_spec = pl.BlockSpec((1, TM, d), lambda i, j: (i, jnp.maximum(j - n_ctx_tiles, 0), 0))
    else:
        h_shape = jax.ShapeDtypeStruct((b, s, d), MXU_DTYPE)
        h_spec = pl.BlockSpec((1, TM, d), tok)
    return pl.pallas_call(
        functools.partial(_combine_kernel, rows=rows, final=final),
        grid=(b, nt),
        in_specs=[pl.BlockSpec((1, TM), lambda i, j: (0, i * nt + j), memory_space=pltpu.SMEM),
                  pl.BlockSpec((1, TM), lambda i, j: (0, jnp.minimum(i * nt + j + 1, b * nt - 1)),
                               memory_space=pltpu.SMEM),
                  pl.BlockSpec(memory_space=pl.ANY),
                  pl.BlockSpec((1, TM, d), tok),
                  pl.BlockSpec((TM, LANES), lambda i, j: (i * nt + j, 0)),
                  pl.BlockSpec((1, 1, SUB, d), msel),
                  pl.BlockSpec((1, 1, SUB, d), msel),
                  pl.BlockSpec((1, d), lambda i, j: (0, 0))],
        out_specs=[pl.BlockSpec((1, TM, d), tok), h_spec],
        out_shape=[jax.ShapeDtypeStruct((b, s, d), F32), h_shape],
        scratch_shapes=[pltpu.VMEM((2, TM * rows, LANES), F32), pltpu.SemaphoreType.DMA((2,))],
        compiler_params=_cparams(("arbitrary", "arbitrary")),
        name="moe_combine_final" if final else "moe_combine",
    )(pos, pos, ys, x, wcol, modsel, modsel_next, g_next.reshape(1, d))


def _grid_sincos(n_tokens, dim):
    rows = n_tokens // GRID_W
    row = jnp.broadcast_to(jnp.arange(rows)[:, None], (rows, GRID_W)).reshape(-1).astype(F32)
    col = jnp.broadcast_to(jnp.arange(GRID_W)[None, :], (rows, GRID_W)).reshape(-1).astype(F32)
    quarter = dim // 4
    freqs = jnp.exp(-math.log(10000.0) * jnp.arange(quarter, dtype=F32) / quarter)
    ar = row[:, None] * freqs
    ac = col[:, None] * freqs
    return jnp.concatenate([jnp.sin(ar), jnp.cos(ar), jnp.sin(ac), jnp.cos(ac)], axis=-1)


def _dft_mats(n, scale):
    idx = jnp.arange(n, dtype=I32)
    ang = ((idx[:, None] * idx[None, :]) % n).astype(F32) * (2.0 * math.pi / n)
    return jnp.cos(ang) * scale, jnp.sin(ang) * scale


def _split_cols(w, sizes):
    out, start = [], 0
    for sz in sizes:
        out.append(w[..., start:start + sz])
        start += sz
    return out


def _layer_weights(d, w_in_l, gla_a_w_l, gla_a_b_l, dt_bias_l, a_log_l, ssd_d_l):
    qk = GLA_HEADS * GLA_DK
    vv = GLA_HEADS * GLA_DV
    inner = SSD_GROUPS * HEADS_PER_GROUP * SSD_HEADDIM
    gn = SSD_GROUPS * SSD_STATE
    nh = SSD_GROUPS * HEADS_PER_GROUP
    sizes = (qk, qk, vv, vv, GLA_RANK, GLA_RANK, d, inner, inner + 2 * gn, nh, nh, d, d, d)
    (wq, wk, wv, wr, wlf, wlb, wu, wz, wxbc, wdf, wdb, wga, wgb, wgc) = _split_cols(w_in_l, sizes)
    w_main = jnp.concatenate([wq, wk, wv, wr, wu, wz, wxbc, wga, wgb, wgc], axis=1).astype(MXU_DTYPE)
    hp = HEADS_PER_GROUP
    pad = lambda w, width: jnp.pad(w, ((0, 0), (0, width - w.shape[1])))
    small = [pad(jnp.concatenate([wlf, wlb], axis=1), LANES)]
    for g in range(SSD_GROUPS):
        small.append(pad(jnp.concatenate([wdf[:, g * hp:(g + 1) * hp], wdb[:, g * hp:(g + 1) * hp]], axis=1), LANES))
    w_small = jnp.concatenate(small, axis=1).astype(MXU_DTYPE)
    aw_pad = jnp.zeros((2, LANES, qk), F32)
    aw_pad = aw_pad.at[0, :GLA_RANK].set(gla_a_w_l[0]).at[1, GLA_RANK:2 * GLA_RANK].set(gla_a_w_l[1])
    aw_pad = aw_pad.astype(MXU_DTYPE)
    ab = gla_a_b_l.reshape(2, 1, qk)
    grp = lambda v: v.reshape(SSD_GROUPS, hp)
    brow = jnp.concatenate([grp(dt_bias_l[0]), grp(dt_bias_l[1]), jnp.zeros((SSD_GROUPS, LANES - 2 * hp), F32)], axis=1)
    arow = jnp.concatenate([grp(a_log_l[0]), grp(a_log_l[1]), jnp.zeros((SSD_GROUPS, LANES - 2 * hp), F32)], axis=1)
    d_exp = jnp.repeat(ssd_d_l, SSD_HEADDIM).reshape(SSD_GROUPS, 1, hp * SSD_HEADDIM)
    return w_main, w_small, aw_pad, ab, brow.reshape(SSD_GROUPS, 1, LANES), arow.reshape(SSD_GROUPS, 1, LANES), d_exp


def kernel(x, c, ctx, c_ctx, ada_w, ada_b, norm1_g, norm2_g, w_in, gla_a_w, gla_a_b, gla_norm_g, gla_proj,
           fnet_proj, ssd_conv_w, ssd_conv_b, ssd_dt_bias, ssd_a_log, ssd_d, ssd_norm_g, ssd_proj, w_out,
           router_w, router_b, exp_w1, exp_w3, exp_w2, final_g):
    bsz, t_lat, d = x.shape
    n_ctx = ctx.shape[1]
    depth = ada_w.shape[0]
    s = n_ctx + t_lat
    n = bsz * s
    nct = n_ctx // TM
    ns = d // LANES

    xs0 = jnp.concatenate([ctx, x + _grid_sincos(t_lat, d)], axis=1)

    rows = 16
    c_all = jnp.concatenate([c, c_ctx[None], jnp.zeros((rows - bsz - 1, d), F32)], axis=0)
    mods = _ada(c_all, ada_w, ada_b)
    mods = mods.reshape(depth, rows, N_MOD, d)
    mods = jnp.pad(mods, ((0, 0), (0, 0), (0, SUB - N_MOD), (0, 0)))
    modsel = jnp.stack([jnp.broadcast_to(mods[:, bsz:bsz + 1], (depth, bsz, SUB, d)), mods[:, :bsz]], axis=2)

    width = d
    gd = width // FNET_GROUPS
    cc, cs_ = _dft_mats(gd, gd ** -0.5)
    w_cs = jnp.concatenate([cc, cs_], axis=1).astype(MXU_DTYPE)
    cos_l, sin_l = (m.astype(MXU_DTYPE) for m in _dft_mats(t_lat, t_lat ** -0.5))
    cos_c, sin_c = (m.astype(MXU_DTYPE) for m in _dft_mats(n_ctx, n_ctx ** -0.5))

    rw_t = router_w.T
    rb = router_b.reshape(N_EXPERTS, 1)
    lo_tab = jnp.array([4 * (bk // N_PAIRS) + PAIR_LO[bk % N_PAIRS] for bk in range(N_BUCKETS)], I32)
    hi_tab = jnp.array([4 * (bk // N_PAIRS) + PAIR_HI[bk % N_PAIRS] for bk in range(N_BUCKETS)], I32)
    cap_rows = n + N_BUCKETS * TMX
    ntile = cap_rows // TMX

    xcur = xs0
    h = _normmod(xcur, norm1_g[0], modsel[0], nct)
    out = None
    for l in range(depth):
        w_main, w_small, aw_pad, ab, brow, arow, d_exp = _layer_weights(
            d, w_in[l], gla_a_w[l], gla_a_b[l], ssd_dt_bias[l], ssd_a_log[l], ssd_d[l])
        p, ps = _inproj(h, w_main, w_small)
        xbc = _conv(p, ssd_conv_w[l], ssd_conv_b[l], n_ctx)
        ga = _gla(p, ps, aw_pad, ab, gla_norm_g[l], n_ctx)
        yz = _ssd(p, xbc, ps, brow, arow, d_exp, n_ctx)
        v = _chan_dft(p, w_cs, width)
        fm = _pos_dft(v, cos_c, sin_c, cos_l, sin_l, width)
        xmid, ht, ri, rf, wcol, cnt = _merge(
            ga, fm, yz, p, xcur, modsel[l], norm2_g[l].reshape(1, d), ssd_norm_g[l].reshape(1, -1),
            gla_proj[l].astype(MXU_DTYPE), fnet_proj[l].astype(MXU_DTYPE), ssd_proj[l].astype(MXU_DTYPE),
            w_out[l].astype(MXU_DTYPE), rw_t, rb, nct)

        counts = cnt[:N_BUCKETS, 0]
        tiles_per = (counts + TMX - 1) // TMX
        tile_end = jnp.cumsum(tiles_per)
        tile_start = tile_end - tiles_per
        bucket = ri[0]
        pos = (jnp.take(tile_start, bucket) * TMX + ri[1]).reshape(1, n)
        tile_bucket = jnp.minimum(jnp.searchsorted(tile_end, jnp.arange(ntile, dtype=I32), side="right"),
                                  N_BUCKETS - 1).astype(I32)
        tile_a = jnp.take(lo_tab, tile_bucket)
        tile_b = jnp.take(hi_tab, tile_bucket)
        nvalid = tile_end[-1:].astype(I32)

        xs_sorted = _dispatch(pos, ht, jnp.zeros((cap_rows * ns, LANES), F32), ns)
        w13 = jnp.concatenate([exp_w1[l], exp_w3[l]], axis=2).astype(MXU_DTYPE)
        ys = _experts(tile_a, tile_b, nvalid, xs_sorted, w13, exp_w2[l].astype(MXU_DTYPE), d)
        final = l == depth - 1
        if final:
            xcur, out = _combine(pos, ys, xmid, wcol, modsel[l], modsel[l], final_g, nct, True)
        else:
            xcur, h = _combine(pos, ys, xmid, wcol, modsel[l], modsel[l + 1], norm1_g[l + 1], nct, False)
    return out
```

```python
import functools
import math

import jax
import jax.numpy as jnp
import numpy as np
from jax import lax
from jax.experimental import pallas as pl
from jax.experimental.pallas import tpu as pltpu

F32 = jnp.float32
BF16 = jnp.bfloat16
I32 = jnp.int32
MXU_DTYPE = BF16

EPS = 1e-6
LOG2E = 1.4426950408889634
GRID_W = 64
N_MOD = 6

GLA_HEADS = 4
GLA_DK = 128
GLA_DV = 256
GLA_RANK = 16
GLA_TAU = 16.0
GLA_CHUNK = 64
GLA_HPS = 2
GLA_BLOCK = 256

FNET_GROUPS = 4

SSD_HEADDIM = 64
SSD_GROUPS = 4
SSD_STATE = 128
SSD_CONV = 5
CONV_EDGE = 16
SSD_CHUNK = 128
HEADS_PER_GROUP = 8

N_EXPERTS = 16
N_EXPERT_GROUPS = 4
EXPERTS_PER_GROUP = 4
D_FF = 512
N_PAIRS = 6
N_BUCKETS = N_EXPERT_GROUPS * N_PAIRS
PAIR_LO = (0, 0, 0, 1, 1, 2)
PAIR_HI = (1, 2, 3, 2, 3, 3)

TM = 256
TMX = 256
TMD = 512
LANES = 128
SUB = 8
VMEM_LIMIT = 56 * 1024 * 1024

COL_Q, COL_K, COL_V, COL_R, COL_U, COL_Z, COL_X, COL_B, COL_C, COL_GA, COL_GB, COL_GC = (
    0, 512, 1024, 2048, 3072, 4096, 6144, 8192, 8704, 9216, 10240, 11264)
P_MAIN = 12288
P_SMALL = LANES * (1 + SSD_GROUPS)
IN_TN = 1024


def _cparams(sem, vmem=VMEM_LIMIT):
    return pltpu.CompilerParams(dimension_semantics=sem, vmem_limit_bytes=vmem)


def _mm(a, b):
    return jnp.dot(a.astype(MXU_DTYPE), b.astype(MXU_DTYPE), preferred_element_type=F32)


def _mm_nt(a, b):
    return lax.dot_general(a.astype(MXU_DTYPE), b.astype(MXU_DTYPE), (((1,), (1,)), ((), ())),
                           preferred_element_type=F32)


def _mm_tn(a, b):
    return lax.dot_general(a.astype(MXU_DTYPE), b.astype(MXU_DTYPE), (((0,), (0,)), ((), ())),
                           preferred_element_type=F32)


def _split3(x):
    hi = x.astype(MXU_DTYPE)
    r1 = x - hi.astype(F32)
    mid = r1.astype(MXU_DTYPE)
    lo = (r1 - mid.astype(F32)).astype(MXU_DTYPE)
    return hi, mid, lo


def _mm_exact_lhs(m01, x):
    hi, mid, lo = _split3(x)
    m = m01.astype(MXU_DTYPE)
    return (jnp.dot(m, hi, preferred_element_type=F32) + jnp.dot(m, mid, preferred_element_type=F32)
            + jnp.dot(m, lo, preferred_element_type=F32))


def _sigmoid(x):
    return 1.0 / (1.0 + jnp.exp(-x))


def _silu(x):
    return x * _sigmoid(x)


def _softplus(x):
    return jnp.maximum(x, 0.0) + jnp.log(1.0 + jnp.exp(-jnp.abs(x)))


def _rms(x, g):
    return x * lax.rsqrt(jnp.mean(x * x, axis=-1, keepdims=True) + EPS) * g


def _modulate(x, g, shift, scale):
    return _rms(x, g) * (1.0 + scale) + shift


def _ada_kernel(c_ref, w_ref, b_ref, o_ref):
    o_ref[0] = _mm(_silu(c_ref[...]), w_ref[0]) + b_ref[0]


def _ada(c_all, ada_w, ada_b):
    depth, d, nd = ada_w.shape
    rows = c_all.shape[0]
    tn = 1024
    return pl.pallas_call(
        _ada_kernel,
        grid=(depth, nd // tn),
        in_specs=[pl.BlockSpec((rows, d), lambda l, j: (0, 0)),
                  pl.BlockSpec((1, d, tn), lambda l, j: (l, 0, j)),
                  pl.BlockSpec((1, 1, tn), lambda l, j: (l, 0, j))],
        out_specs=pl.BlockSpec((1, rows, tn), lambda l, j: (l, 0, j)),
        out_shape=jax.ShapeDtypeStruct((depth, rows, nd), F32),
        compiler_params=_cparams(("arbitrary", "arbitrary")),
        name="ada_mod",
    )(c_all, ada_w, ada_b.reshape(depth, 1, nd))


def _normmod_kernel(x_ref, g_ref, mod_ref, h_ref):
    mod = mod_ref[0, 0]
    h_ref[0] = _modulate(x_ref[0], g_ref[...], mod[0:1], mod[1:2]).astype(h_ref.dtype)


def _normmod(x, g, modsel, n_ctx_tiles):
    b, s, d = x.shape
    return pl.pallas_call(
        _normmod_kernel,
        grid=(b, s // TM),
        in_specs=[pl.BlockSpec((1, TM, d), lambda i, j: (i, j, 0)),
                  pl.BlockSpec((1, d), lambda i, j: (0, 0)),
                  pl.BlockSpec((1, 1, SUB, d), lambda i, j: (i, jnp.where(j < n_ctx_tiles, 0, 1), 0, 0))],
        out_specs=pl.BlockSpec((1, TM, d), lambda i, j: (i, j, 0)),
        out_shape=jax.ShapeDtypeStruct((b, s, d), MXU_DTYPE),
        compiler_params=_cparams(("arbitrary", "arbitrary")),
        name="norm_mod",
    )(x, g.reshape(1, d), modsel)


def _inproj_kernel(h_ref, w_ref, ws_ref, p_ref, ps_ref):
    h = h_ref[0]
    p_ref[0] = _mm(h, w_ref[...]).astype(p_ref.dtype)

    @pl.when(pl.program_id(1) == 0)
    def _():
        ps_ref[0] = _mm(h, ws_ref[...])


def _inproj(h, w_main, w_small):
    b, s, d = h.shape
    return pl.pallas_call(
        _inproj_kernel,
        grid=(b, P_MAIN // IN_TN),
        in_specs=[pl.BlockSpec((1, s, d), lambda i, j: (i, 0, 0)),
                  pl.BlockSpec((d, IN_TN), lambda i, j: (0, j)),
                  pl.BlockSpec((d, P_SMALL), lambda i, j: (0, 0))],
        out_specs=[pl.BlockSpec((1, s, IN_TN), lambda i, j: (i, 0, j)),
                   pl.BlockSpec((1, s, P_SMALL), lambda i, j: (i, 0, 0))],
        out_shape=[jax.ShapeDtypeStruct((b, s, P_MAIN), MXU_DTYPE),
                   jax.ShapeDtypeStruct((b, s, P_SMALL), F32)],
        compiler_params=_cparams(("arbitrary", "arbitrary")),
        name="in_proj",
    )(h, w_main, w_small)


def _conv_kernel(p_ref, w_ref, b_ref, o_ref, *, n_ctx):
    x = p_ref[0].astype(F32)
    s = x.shape[0]
    w = w_ref[...]
    bias = b_ref[...]
    half = SSD_CONV // 2

    def taps(xw, row0, masked):
        r = xw.shape[0]
        acc = jnp.zeros_like(xw) + bias
        if masked:
            row = row0 + lax.broadcasted_iota(I32, (r, 1), 0)
            seg_lo = jnp.where(row < n_ctx, 0, n_ctx)
            seg_hi = jnp.where(row < n_ctx, n_ctx, s)
        for k in range(SSD_CONV):
            dlt = k - half
            xs = xw if dlt == 0 else pltpu.roll(xw, shift=(-dlt) % r, axis=0)
            if masked:
                t2 = row + dlt
                ok = jnp.broadcast_to(jnp.where((t2 >= seg_lo) & (t2 < seg_hi), 1.0, 0.0), xw.shape)
                xs = jnp.where(ok > 0.5, xs, 0.0)
            acc = acc + w[k:k + 1] * xs
        return _silu(acc)

    o_ref[0] = taps(x, 0, False).astype(o_ref.dtype)
    eb, win = CONV_EDGE, 2 * CONV_EDGE
    for r0 in (0, n_ctx - eb, n_ctx, s - eb):
        a = min(max(r0 - SUB, 0), s - win)
        o_ref[0, r0:r0 + eb, :] = taps(x[a:a + win], a, True)[r0 - a:r0 - a + eb].astype(o_ref.dtype)


def _conv(p, conv_w, conv_b, n_ctx):
    b, s, _ = p.shape
    c = conv_w.shape[1]
    tc = 256
    return pl.pallas_call(
        functools.partial(_conv_kernel, n_ctx=n_ctx),
        grid=(b, c // tc),
        in_specs=[pl.BlockSpec((1, s, tc), lambda i, j: (i, 0, COL_X // tc + j)),
                  pl.BlockSpec((SSD_CONV, tc), lambda i, j: (0, j)),
                  pl.BlockSpec((1, tc), lambda i, j: (0, j))],
        out_specs=pl.BlockSpec((1, s, tc), lambda i, j: (i, 0, j)),
        out_shape=jax.ShapeDtypeStruct((b, s, c), MXU_DTYPE),
        compiler_params=_cparams(("arbitrary", "arbitrary")),
        name="ssd_conv",
    )(p, conv_w, conv_b.reshape(1, c))


def _gla_kernel(q_ref, k_ref, v_ref, r_ref, ps_ref, aw_ref, ab_ref, g_ref, o_ref,
                la_s, od_s, st_s, *, n_ctx):
    L = GLA_CHUNK
    R = GLA_BLOCK
    cpb = R // L
    dk, dv = GLA_DK, GLA_DV
    s = q_ref.shape[1]
    nb = s // R
    nbc = n_ctx // R
    lr = ps_ref[0]
    for d in range(2):
        pre = _mm(lr, aw_ref[d]) + ab_ref[d]
        la_s[d] = -_softplus(-pre) * (1.0 / GLA_TAU)

    ri = lax.broadcasted_iota(I32, (R, R), 0)
    ci = lax.broadcasted_iota(I32, (R, R), 1)
    same = (ri // L) == (ci // L)
    scale = dk ** -0.5
    st_s[...] = jnp.zeros_like(st_s)

    def per_chunk(x, idx):
        x3 = x.reshape(cpb, L, x.shape[-1])
        return jnp.broadcast_to(x3[:, idx:idx + 1, :], x3.shape).reshape(x.shape)

    def block(off, hh, reverse, st, keep, bcum_all):
        q = q_ref[0, pl.ds(off, R), hh * dk:(hh + 1) * dk].astype(F32) * scale
        k = k_ref[0, pl.ds(off, R), hh * dk:(hh + 1) * dk].astype(F32)
        v = v_ref[0, pl.ds(off, R), hh * dv:(hh + 1) * dv]
        bcum = bcum_all[:, hh * dk:(hh + 1) * dk]
        b_mid = per_chunk(bcum, L // 2 if reverse else L // 2 - 1)
        b_last = per_chunk(bcum, 0 if reverse else L - 1)
        sc = _mm_nt(q * jnp.exp(bcum - b_mid), k * jnp.exp(b_mid - bcum))
        o_intra = _mm(jnp.where(keep, sc, 0.0), v)
        qb = (q * jnp.exp(bcum)).astype(MXU_DTYPE)
        kd = (k * jnp.exp(b_last - bcum)).astype(MXU_DTYPE)
        dec = jnp.exp(b_last)
        o_inter = [None] * cpb
        for c in (range(cpb - 1, -1, -1) if reverse else range(cpb)):
            rows = slice(c * L, (c + 1) * L)
            o_inter[c] = _mm_nt(qb[rows], st)
            st = dec[c * L:c * L + 1] * st + _mm_tn(v[rows], kd[rows])
        return o_intra + jnp.concatenate(o_inter, axis=0), st

    @pl.loop(0, nb)
    def _(i):
        off_f = pl.multiple_of(i * R, R)
        jb = jnp.where(i < nbc, nbc - 1 - i, nb + nbc - 1 - i)
        off_b = pl.multiple_of(jb * R, R)
        offs = (off_f, off_b)
        keeps = (same & (ci <= ri), same & (ci >= ri))
        bcums = [_mm_exact_lhs(jnp.where(keeps[d], 1.0, 0.0), la_s[d, pl.ds(offs[d], R), :]) for d in range(2)]
        chains = [(d, hh) for hh in range(GLA_HPS) for d in range(2)]
        states = [st_s[d, hh] for d, hh in chains]
        res = [block(offs[d], hh, bool(d), st, keeps[d], bcums[d]) for (d, hh), st in zip(chains, states)]
        for (d, hh), (o, st) in zip(chains, res):
            od_s[d, pl.ds(off_b if d else off_f, R), hh * dv:(hh + 1) * dv] = o
            st_s[d, hh] = st

    g = g_ref[...]

    @pl.loop(0, s // L)
    def _(c):
        off = pl.multiple_of(c * L, L)
        o = od_s[0, pl.ds(off, L), :] + od_s[1, pl.ds(off, L), :]
        r = r_ref[0, pl.ds(off, L), :].astype(F32)
        outs = [_rms(o[:, hh * dv:(hh + 1) * dv], g) for hh in range(GLA_HPS)]
        o_ref[0, pl.ds(off, L), :] = (jnp.concatenate(outs, axis=1) * _silu(r)).astype(o_ref.dtype)


def _gla(p, ps, aw_pad, ab, norm_g, n_ctx):
    b, s, _ = p.shape
    dk, dv = GLA_HPS * GLA_DK, GLA_HPS * GLA_DV
    return pl.pallas_call(
        functools.partial(_gla_kernel, n_ctx=n_ctx),
        grid=(b, GLA_HEADS // GLA_HPS),
        in_specs=[pl.BlockSpec((1, s, dk), lambda i, h: (i, 0, COL_Q // dk + h)),
                  pl.BlockSpec((1, s, dk), lambda i, h: (i, 0, COL_K // dk + h)),
                  pl.BlockSpec((1, s, dv), lambda i, h: (i, 0, COL_V // dv + h)),
                  pl.BlockSpec((1, s, dv), lambda i, h: (i, 0, COL_R // dv + h)),
                  pl.BlockSpec((1, s, LANES), lambda i, h: (i, 0, 0)),
                  pl.BlockSpec((2, LANES, dk), lambda i, h: (0, 0, h)),
                  pl.BlockSpec((2, 1, dk), lambda i, h: (0, 0, h)),
                  pl.BlockSpec((1, GLA_DV), lambda i, h: (0, 0))],
        out_specs=pl.BlockSpec((1, s, dv), lambda i, h: (i, 0, h)),
        out_shape=jax.ShapeDtypeStruct((b, s, GLA_HEADS * GLA_DV), MXU_DTYPE),
        scratch_shapes=[pltpu.VMEM((2, s, dk), F32), pltpu.VMEM((2, s, dv), F32),
                        pltpu.VMEM((2, GLA_HPS, GLA_DV, GLA_DK), F32)],
        compiler_params=_cparams(("arbitrary", "arbitrary")),
        name="gla",
    )(p, p, p, p, ps, aw_pad, ab, norm_g.reshape(1, GLA_DV))


def _ssd_kernel(z_ref, x_ref, bm_ref, cm_ref, ps_ref, brow_ref, alog_ref, d_ref, o_ref,
                dt_s, acs_s, rt_s, cb_s, y_s, st_s, *, n_ctx):
    L = SSD_CHUNK
    s = x_ref.shape[1]
    n = s // L
    nc = n_ctx // L
    hp = HEADS_PER_GROUP
    npair = hp // 2
    neg = -1e30
    dt_s[...] = _softplus(ps_ref[0] + brow_ref[0])
    a_row = -jnp.exp(alog_ref[0])

    ri = lax.broadcasted_iota(I32, (L, L), 0)
    ci = lax.broadcasted_iota(I32, (L, L), 1)
    lane = lax.broadcasted_iota(I32, (1, LANES), 1)
    low = lane < SSD_HEADDIM
    keeps = (ci <= ri, ci >= ri)
    er = lax.broadcasted_iota(I32, (LANES, hp * SSD_HEADDIM), 0)
    ec = lax.broadcasted_iota(I32, (LANES, hp * SSD_HEADDIM), 1) // SSD_HEADDIM
    esel = [jnp.where(er == d * hp + ec, 1.0, 0.0).astype(MXU_DTYPE) for d in range(2)]

    @pl.loop(0, n, unroll=2)
    def _(c):
        off = pl.multiple_of(c * L, L)
        dtc = dt_s[pl.ds(off, L), :]
        wc = dtc * a_row
        pre = _mm_exact_lhs(jnp.where(keeps[0], 1.0, 0.0), wc)
        suf = pre[L - 1:L] - pre + wc
        acs = jnp.where(lane < hp, pre, suf)
        acs_s[pl.ds(off, L), :] = acs * LOG2E
        rt_s[pl.ds(off, L), :] = ((acs - jnp.log(dtc)) * LOG2E).T
        cb_s[pl.ds(off, L), :] = _mm_nt(cm_ref[0, pl.ds(off, L), :], bm_ref[0, pl.ds(off, L), :])

    st_s[...] = jnp.zeros_like(st_s)

    def chunk(off, d, st):
        base = d * hp
        keep = keeps[d]
        i_last = 0 if d else L - 1
        xb = x_ref[0, pl.ds(off, L), :]
        bm = bm_ref[0, pl.ds(off, L), :]
        cm = cm_ref[0, pl.ds(off, L), :]
        dtc = dt_s[pl.ds(off, L), :]
        acs = acs_s[pl.ds(off, L), :]
        rt = rt_s[pl.ds(off, L), :]
        cb = cb_s[pl.ds(off, L), :]
        mine = (lane >= base) & (lane < base + hp)
        to_end = dtc * jnp.exp2(jnp.where(mine, acs[i_last:i_last + 1] - acs, 0.0))
        hi = to_end.astype(MXU_DTYPE)
        lo = (to_end - hi.astype(F32)).astype(MXU_DTYPE)
        to_end_e = (jnp.dot(hi, esel[d], preferred_element_type=F32)
                    + jnp.dot(lo, esel[d], preferred_element_type=F32))
        y_off = _mm(cm, st)
        ys, decs = [], []
        for pr in range(npair):
            a0 = acs[:, base + 2 * pr:base + 2 * pr + 1]
            a1 = acs[:, base + 2 * pr + 1:base + 2 * pr + 2]
            m2 = jnp.concatenate(
                [cb * jnp.exp2(jnp.where(keep, a0 - rt[base + 2 * pr:base + 2 * pr + 1, :], neg)),
                 cb * jnp.exp2(jnp.where(keep, a1 - rt[base + 2 * pr + 1:base + 2 * pr + 2, :], neg))], axis=1)
            xp = xb[:, pr * LANES:(pr + 1) * LANES]
            zero = jnp.zeros_like(xp)
            bd = jnp.concatenate([jnp.where(low, xp, zero), jnp.where(low, zero, xp)], axis=0)
            dec = jnp.exp2(jnp.where(low, a0, a1))
            ys.append(_mm(m2, bd) + y_off[:, pr * LANES:(pr + 1) * LANES] * dec)
            decs.append(dec[i_last:i_last + 1])
        st = jnp.concatenate(decs, axis=1) * st + _mm_tn(bm, xb.astype(F32) * to_end_e)
        return jnp.concatenate(ys, axis=1), st

    @pl.loop(0, n)
    def _(i):
        jb = jnp.where(i < nc, nc - 1 - i, n + nc - 1 - i)
        offs = (pl.multiple_of(i * L, L), pl.multiple_of(jb * L, L))
        states = [st_s[d] for d in range(2)]
        res = [chunk(offs[d], d, states[d]) for d in range(2)]
        for d, (y, st) in enumerate(res):
            y_s[d, pl.ds(offs[d], L), :] = y
            st_s[d] = st

    dsk = d_ref[0]

    @pl.loop(0, n)
    def _(c):
        off = pl.multiple_of(c * L, L)
        y = y_s[0, pl.ds(off, L), :] + y_s[1, pl.ds(off, L), :] + dsk * x_ref[0, pl.ds(off, L), :].astype(F32)
        z = z_ref[0, pl.ds(off, L), :].astype(F32)
        o_ref[0, pl.ds(off, L), :] = (y * _silu(z)).astype(o_ref.dtype)


def _ssd(p, xbc, ps, brow, alog_row, d_exp, n_ctx):
    b, s, _ = p.shape
    gw = HEADS_PER_GROUP * SSD_HEADDIM
    inner = SSD_GROUPS * gw
    ns = SSD_STATE
    return pl.pallas_call(
        functools.partial(_ssd_kernel, n_ctx=n_ctx),
        grid=(b, SSD_GROUPS),
        in_specs=[pl.BlockSpec((1, s, gw), lambda i, g: (i, 0, COL_Z // gw + g)),
                  pl.BlockSpec((1, s, gw), lambda i, g: (i, 0, g)),
                  pl.BlockSpec((1, s, ns), lambda i, g: (i, 0, inner // ns + g)),
                  pl.BlockSpec((1, s, ns), lambda i, g: (i, 0, inner // ns + SSD_GROUPS + g)),
                  pl.BlockSpec((1, s, LANES), lambda i, g: (i, 0, 1 + g)),
                  pl.BlockSpec((1, 1, LANES), lambda i, g: (g, 0, 0)),
                  pl.BlockSpec((1, 1, LANES), lambda i, g: (g, 0, 0)),
                  pl.BlockSpec((1, 1, gw), lambda i, g: (g, 0, 0))],
        out_specs=pl.BlockSpec((1, s, gw), lambda i, g: (i, 0, g)),
        out_shape=jax.ShapeDtypeStruct((b, s, inner), MXU_DTYPE),
        scratch_shapes=[pltpu.VMEM((s, LANES), F32), pltpu.VMEM((s, LANES), F32),
                        pltpu.VMEM((s, LANES), F32), pltpu.VMEM((s, LANES), F32),
                        pltpu.VMEM((2, s, gw), F32), pltpu.VMEM((2, ns, gw), F32)],
        compiler_params=_cparams(("arbitrary", "arbitrary")),
        name="ssd",
    )(p, xbc, xbc, xbc, ps, brow, alog_row, d_exp)


def _chan_dft_kernel(u_ref, w_ref, o_ref):
    u = u_ref[0]
    gd = w_ref.shape[0]
    w = w_ref[...]
    cs, sn = [], []
    for g in range(FNET_GROUPS):
        r = _mm(u[:, g * gd:(g + 1) * gd], w)
        cs.append(r[:, :gd])
        sn.append(r[:, gd:])
    o_ref[0] = jnp.concatenate(cs + sn, axis=1).astype(o_ref.dtype)


def _chan_dft(p, w_cs, width):
    b, s, _ = p.shape
    gd = width // FNET_GROUPS
    return pl.pallas_call(
        _chan_dft_kernel,
        grid=(b, s // TM),
        in_specs=[pl.BlockSpec((1, TM, width), lambda i, j: (i, j, COL_U // width)),
                  pl.BlockSpec((gd, 2 * gd), lambda i, j: (0, 0))],
        out_specs=pl.BlockSpec((1, TM, 2 * width), lambda i, j: (i, j, 0)),
        out_shape=jax.ShapeDtypeStruct((b, s, 2 * width), MXU_DTYPE),
        compiler_params=_cparams(("arbitrary", "arbitrary")),
        name="fnet_chan",
    )(p, w_cs)


def _pos_dft_ctx_kernel(c_ref, s_ref, uc_ref, us_ref, o_ref):
    o_ref[0] = (_mm(c_ref[...], uc_ref[0]) - _mm(s_ref[...], us_ref[0])).astype(o_ref.dtype)


def _pos_dft_ctx(v, cos_c, sin_c, width):
    b = v.shape[0]
    n_ctx = cos_c.shape[0]
    return pl.pallas_call(
        _pos_dft_ctx_kernel,
        grid=(b, n_ctx // TM),
        in_specs=[pl.BlockSpec((TM, n_ctx), lambda i, j: (j, 0)),
                  pl.BlockSpec((TM, n_ctx), lambda i, j: (j, 0)),
                  pl.BlockSpec((1, n_ctx, width), lambda i, j: (i, 0, 0)),
                  pl.BlockSpec((1, n_ctx, width), lambda i, j: (i, 0, 1))],
        out_specs=pl.BlockSpec((1, TM, width), lambda i, j: (i, j, 0)),
        out_shape=jax.ShapeDtypeStruct((b, n_ctx, width), MXU_DTYPE),
        compiler_params=_cparams(("arbitrary", "arbitrary")),
        name="fnet_pos_ctx",
    )(cos_c, sin_c, v, v)


def _pos_dft_lat_kernel(cw_ref, sw_ref, jr_ref, uc_ref, us_ref, o_ref, *, n_ctx):
    a = _mm(cw_ref[0], uc_ref[0, n_ctx:, :])
    bm = _mm(sw_ref[0], us_ref[0, n_ctx:, :])
    o_ref[0, :TM, :] = (a - bm)[:TM].astype(o_ref.dtype)
    o_ref[0, TM:, :] = _mm(jr_ref[...], a + bm).astype(o_ref.dtype)


def _pos_dft_lat(v, cos_w, sin_w, jrev, n_ctx, width):
    b, s, _ = v.shape
    nh, wr, t = cos_w.shape
    return pl.pallas_call(
        functools.partial(_pos_dft_lat_kernel, n_ctx=n_ctx),
        grid=(b, nh),
        in_specs=[pl.BlockSpec((1, wr, t), lambda i, j: (j, 0, 0)),
                  pl.BlockSpec((1, wr, t), lambda i, j: (j, 0, 0)),
                  pl.BlockSpec((TM, wr), lambda i, j: (0, 0)),
                  pl.BlockSpec((1, s, width), lambda i, j: (i, 0, 0)),
                  pl.BlockSpec((1, s, width), lambda i, j: (i, 0, 1))],
        out_specs=pl.BlockSpec((1, 2 * TM, width), lambda i, j: (i, j, 0)),
        out_shape=jax.ShapeDtypeStruct((b, t, width), MXU_DTYPE),
        compiler_params=_cparams(("arbitrary", "arbitrary")),
        name="fnet_pos",
    )(cos_w, sin_w, jrev, v, v)


def _merge_kernel(ga_ref, fmc_ref, fml_ref, yz_ref, g1_ref, g2_ref, g3_ref, x_ref, mod_ref, n2g_ref, sg_ref,
                  wa_ref, wb_ref, wc_ref, wo_ref, rw_ref, rb_ref,
                  xo_ref, ht_ref, ri_ref, rf_ref, wcol_ref, cnt_ref, carry_s, *, n_ctx_tiles):
    first = (pl.program_id(0) == 0) & (pl.program_id(1) == 0)

    @pl.when(first)
    def _():
        carry_s[...] = jnp.zeros_like(carry_s)

    mod = mod_ref[0, 0]
    ya = _mm(ga_ref[0], wa_ref[...])
    fm = jnp.where(pl.program_id(1) < n_ctx_tiles, fmc_ref[0], fml_ref[0])
    yb = _mm(fm, wb_ref[...])
    yz = yz_ref[0].astype(F32)
    yc = _mm(_rms(yz, sg_ref[...]), wc_ref[...])
    y = (_sigmoid(g1_ref[0].astype(F32)) * ya + _sigmoid(g2_ref[0].astype(F32)) * yb
         + _sigmoid(g3_ref[0].astype(F32)) * yc)
    xn = x_ref[0] + mod[2:3] * _mm(y, wo_ref[...])
    xo_ref[0] = xn
    h2 = _modulate(xn, n2g_ref[...], mod[3:4], mod[4:5])
    d = h2.shape[1]
    tm = h2.shape[0]
    for sidx in range(d // LANES):
        ht_ref[pl.ds(sidx, tm, stride=d // LANES), :] = h2[:, sidx * LANES:(sidx + 1) * LANES]

    hh, hm, _hl = _split3(h2)
    rh, rm, _rl = _split3(rw_ref[...])
    nt = (((1,), (1,)), ((), ()))
    lg = (lax.dot_general(rh, hh, nt, preferred_element_type=F32)
          + lax.dot_general(rh, hm, nt, preferred_element_type=F32)
          + lax.dot_general(rm, hh, nt, preferred_element_type=F32))
    sc = _sigmoid(lg)
    sel = sc + rb_ref[...]
    srow = [sel[e:e + 1] for e in range(N_EXPERTS)]
    urow = [sc[e:e + 1] for e in range(N_EXPERTS)]
    epg = EXPERTS_PER_GROUP

    best_v = best_pair = best_g = best_slo = best_shi = None
    for g in range(N_EXPERT_GROUPS):
        a = srow[g * epg:(g + 1) * epg]
        u = urow[g * epg:(g + 1) * epg]
        gs = None
        for i in range(epg):
            for j in range(i + 1, epg):
                ps_ = a[i] + a[j]
                gs = ps_ if gs is None else jnp.maximum(gs, ps_)
        m1, i1 = a[0], jnp.zeros_like(a[0], dtype=I32)
        for i in range(1, epg):
            up = a[i] > m1
            i1 = jnp.where(up, i, i1)
            m1 = jnp.where(up, a[i], m1)
        m2 = jnp.full_like(a[0], -jnp.inf)
        i2 = jnp.full_like(i1, -1)
        for i in range(epg):
            cand = (i1 != i) & (a[i] > m2)
            i2 = jnp.where(cand, i, i2)
            m2 = jnp.where(cand, a[i], m2)
        lo = jnp.minimum(i1, i2)
        hi = jnp.maximum(i1, i2)
        pair = jnp.where(lo == 0, 0, jnp.where(lo == 1, 3, 5)) + (hi - lo - 1)
        s_lo = jnp.where(lo == 0, u[0], jnp.where(lo == 1, u[1], u[2]))
        s_hi = jnp.where(hi == 1, u[1], jnp.where(hi == 2, u[2], u[3]))
        if g == 0:
            best_v, best_pair, best_g, best_slo, best_shi = gs, pair, jnp.zeros_like(pair), s_lo, s_hi
        else:
            up = gs > best_v
            best_v = jnp.where(up, gs, best_v)
            best_pair = jnp.where(up, pair, best_pair)
            best_g = jnp.where(up, g, best_g)
            best_slo = jnp.where(up, s_lo, best_slo)
            best_shi = jnp.where(up, s_hi, best_shi)
    bucket = best_g * N_PAIRS + best_pair
    tot = best_slo + best_shi
    w_lo = best_slo / tot
    w_hi = best_shi / tot

    nb = carry_s.shape[0]
    oh = (lax.broadcasted_iota(I32, (nb, tm), 0) == bucket).astype(F32)
    tri = (lax.broadcasted_iota(I32, (tm, tm), 0) <= lax.broadcasted_iota(I32, (tm, tm), 1))
    csum = _mm(oh, jnp.where(tri, 1.0, 0.0))
    carry = carry_s[...]
    rank = jnp.sum(oh * (csum - 1.0 + carry), axis=0, keepdims=True)
    carry = carry + jnp.sum(oh, axis=1, keepdims=True)
    carry_s[...] = carry
    cnt_ref[...] = jnp.broadcast_to(carry, cnt_ref.shape).astype(I32)

    zi = jnp.zeros((SUB - 2, tm), I32)
    ri_ref[...] = jnp.concatenate([bucket, rank.astype(I32), zi], axis=0)
    wrows = jnp.concatenate([w_lo, w_hi, jnp.zeros((SUB - 2, tm), F32)], axis=0)
    rf_ref[...] = wrows
    eye = (lax.broadcasted_iota(I32, (SUB, LANES), 0) == lax.broadcasted_iota(I32, (SUB, LANES), 1))
    eye = jnp.where(eye, 1.0, 0.0).astype(MXU_DTYPE)
    tn = (((0,), (0,)), ((), ()))
    p1, p2, p3 = _split3(wrows)
    wcol_ref[...] = (lax.dot_general(p1, eye, tn, preferred_element_type=F32)
                     + lax.dot_general(p2, eye, tn, preferred_element_type=F32)
                     + lax.dot_general(p3, eye, tn, preferred_element_type=F32))


def _merge(ga, fmc, fml, yz, p, x, modsel, n2g, ssd_g, w_a, w_b, w_c, w_o, rw_t, rb, n_ctx_tiles):
    b, s, d = x.shape
    nt = s // TM
    n = b * s
    di = yz.shape[2]
    nbp = 32
    tok = lambda i, j: (i, j, 0)
    flat = lambda i, j: (0, i * nt + j)
    const = lambda i, j: (0, 0)
    nct = n_ctx_tiles
    nh = (nt - nct) // 2

    def fml_idx(i, j):
        m = jnp.maximum(j - nct, 0)
        return (i, jnp.where(m < nh, 2 * m, 2 * (2 * nh - 1 - m) + 1), 0)

    return pl.pallas_call(
        functools.partial(_merge_kernel, n_ctx_tiles=nct),
        grid=(b, nt),
        in_specs=[pl.BlockSpec((1, TM, d), tok),
                  pl.BlockSpec((1, TM, d), lambda i, j: (i, jnp.minimum(j, nct - 1), 0)),
                  pl.BlockSpec((1, TM, d), fml_idx),
                  pl.BlockSpec((1, TM, di), tok),
                  pl.BlockSpec((1, TM, d), lambda i, j: (i, j, COL_GA // d)),
                  pl.BlockSpec((1, TM, d), lambda i, j: (i, j, COL_GB // d)),
                  pl.BlockSpec((1, TM, d), lambda i, j: (i, j, COL_GC // d)),
                  pl.BlockSpec((1, TM, d), tok),
                  pl.BlockSpec((1, 1, SUB, d), lambda i, j: (i, jnp.where(j < n_ctx_tiles, 0, 1), 0, 0)),
                  pl.BlockSpec((1, d), const),
                  pl.BlockSpec((1, di), const),
                  pl.BlockSpec((d, d), const),
                  pl.BlockSpec((d, d), const),
                  pl.BlockSpec((di, d), const),
                  pl.BlockSpec((d, d), const),
                  pl.BlockSpec((N_EXPERTS, d), const),
                  pl.BlockSpec((N_EXPERTS, 1), const)],
        out_specs=[pl.BlockSpec((1, TM, d), tok),
                   pl.BlockSpec((TM * (d // LANES), LANES), lambda i, j: (i * nt + j, 0)),
                   pl.BlockSpec((SUB, TM), flat),
                   pl.BlockSpec((SUB, TM), flat),
                   pl.BlockSpec((TM, LANES), lambda i, j: (i * nt + j, 0)),
                   pl.BlockSpec((nbp, LANES), const)],
        out_shape=[jax.ShapeDtypeStruct((b, s, d), F32),
                   jax.ShapeDtypeStruct((n * (d // LANES), LANES), F32),
                   jax.ShapeDtypeStruct((SUB, n), I32),
                   jax.ShapeDtypeStruct((SUB, n), F32),
                   jax.ShapeDtypeStruct((n, LANES), F32),
                   jax.ShapeDtypeStruct((nbp, LANES), I32)],
        scratch_shapes=[pltpu.VMEM((nbp, 1), F32)],
        compiler_params=_cparams(("arbitrary", "arbitrary")),
        name="merge_route",
    )(ga, fmc, fml, yz, p, p, p, x, modsel, n2g, ssd_g, w_a, w_b, w_c, w_o, rw_t, rb)


def _row_copy(src, dst, src_row, dst_row, rows, sem):
    def at(ref, r):
        start = r * rows if isinstance(r, int) else pl.multiple_of(r * rows, rows)
        return ref.at[pl.ds(start, rows)]
    return pltpu.make_async_copy(at(src, src_row), at(dst, dst_row), sem)


def _dispatch_kernel(pos_ref, ht_ref, xs_in_ref, xs_ref, sem, *, rows):
    del xs_in_ref
    copies = [_row_copy(ht_ref, xs_ref, t, pos_ref[0, t], rows, sem) for t in range(TMD)]
    for cp in copies:
        cp.start()
    for cp in copies:
        cp.wait()


def _dispatch(pos, ht, xs_init, rows):
    n = pos.shape[1]
    return pl.pallas_call(
        functools.partial(_dispatch_kernel, rows=rows),
        grid=(n // TMD,),
        in_specs=[pl.BlockSpec((1, TMD), lambda i: (0, i), memory_space=pltpu.SMEM),
                  pl.BlockSpec((TMD * rows, LANES), lambda i: (i, 0)),
                  pl.BlockSpec(memory_space=pl.ANY)],
        out_specs=pl.BlockSpec(memory_space=pl.ANY),
        out_shape=jax.ShapeDtypeStruct(xs_init.shape, xs_init.dtype),
        input_output_aliases={2: 0},
        scratch_shapes=[pltpu.SemaphoreType.DMA(())],
        compiler_params=_cparams(("arbitrary",)),
        name="moe_dispatch",
    )(pos, ht, xs_init)


def _expert_kernel(ta_ref, tb_ref, nv_ref, xs_ref, w1a_ref, w3a_ref, w2a_ref, w1b_ref, w3b_ref, w2b_ref, ys_ref):
    del ta_ref, tb_ref

    @pl.when(pl.program_id(0) >= nv_ref[0])
    def _():
        ys_ref[...] = jnp.zeros_like(ys_ref)

    @pl.when(pl.program_id(0) < nv_ref[0])
    def _():
        ns = xs_ref.shape[0] // TMX
        x = jnp.concatenate([xs_ref[pl.ds(sidx, TMX, stride=ns), :] for sidx in range(ns)], axis=1)
        x = x.astype(MXU_DTYPE)
        for kk, (w1_ref, w3_ref, w2_ref) in enumerate(((w1a_ref, w3a_ref, w2a_ref), (w1b_ref, w3b_ref, w2b_ref))):
            he = _silu(_mm(x, w1_ref[0])) * _mm(x, w3_ref[0])
            y = _mm(he, w2_ref[0])
            for sidx in range(ns):
                ys_ref[pl.ds(kk * ns + sidx, TMX, stride=2 * ns), :] = y[:, sidx * LANES:(sidx + 1) * LANES]


def _experts(tile_a, tile_b, nvalid, xs, w1, w3, w2, d):
    ns = d // LANES
    ntile = xs.shape[0] // (TMX * ns)
    last = lambda i, nv: jnp.minimum(i, nv[0] - 1)
    up = lambda t: pl.BlockSpec((1, d, D_FF), lambda i, ta, tb, nv: ((ta, tb)[t][last(i, nv)], 0, 0))
    down = lambda t: pl.BlockSpec((1, D_FF, d), lambda i, ta, tb, nv: ((ta, tb)[t][last(i, nv)], 0, 0))
    grid_spec = pltpu.PrefetchScalarGridSpec(
        num_scalar_prefetch=3,
        grid=(ntile,),
        in_specs=[pl.BlockSpec((TMX * ns, LANES), lambda i, ta, tb, nv: (last(i, nv), 0)),
                  up(0), up(0), down(0), up(1), up(1), down(1)],
        out_specs=pl.BlockSpec((TMX * 2 * ns, LANES), lambda i, ta, tb, nv: (i, 0)),
    )
    return pl.pallas_call(
        _expert_kernel,
        grid_spec=grid_spec,
        out_shape=jax.ShapeDtypeStruct((ntile * TMX * 2 * ns, LANES), F32),
        compiler_params=_cparams(("arbitrary",)),
        name="moe_experts",
    )(tile_a, tile_b, nvalid, xs, w1, w3, w2, w1, w3, w2)


def _combine_kernel(pos_ref, posn_ref, ys_ref, x_ref, wcol_ref, mod_ref, modn_ref, g_ref, xo_ref, ho_ref, buf, sem,
                    *, rows, final):
    step = pl.program_id(0) * pl.num_programs(1) + pl.program_id(1)
    last = pl.num_programs(0) * pl.num_programs(1) - 1
    slot = step % 2

    def gathers(p_ref, sl):
        return [_row_copy(ys_ref, buf.at[sl], p_ref[0, t], t, rows, sem.at[sl]) for t in range(TM)]

    @pl.when(step == 0)
    def _():
        for cp in gathers(pos_ref, 0):
            cp.start()

    @pl.when(step < last)
    def _():
        for cp in gathers(posn_ref, 1 - slot):
            cp.start()

    for cp in gathers(pos_ref, slot):
        cp.wait()

    ns = rows // 2
    ya = jnp.concatenate([buf[slot, pl.ds(sidx, TM, stride=rows), :] for sidx in range(ns)], axis=1)
    yb = jnp.concatenate([buf[slot, pl.ds(ns + sidx, TM, stride=rows), :] for sidx in range(ns)], axis=1)
    wc = wcol_ref[...]
    y = wc[:, 0:1] * ya + wc[:, 1:2] * yb
    mod = mod_ref[0, 0]
    xn = x_ref[0] + mod[5:6] * y
    xo_ref[0] = xn
    if final:
        ho_ref[0] = _rms(xn, g_ref[...]).astype(ho_ref.dtype)
    else:
        modn = modn_ref[0, 0]
        ho_ref[0] = _modulate(xn, g_ref[...], modn[0:1], modn[1:2]).astype(ho_ref.dtype)


def _combine(pos, ys, x, wcol, modsel, modsel_next, g_next, n_ctx_tiles, final):
    b, s, d = x.shape
    nt = s // TM
    rows = 2 * (d // LANES)
    tok = lambda i, j: (i, j, 0)
    msel = lambda i, j: (i, jnp.where(j < n_ctx_tiles, 0, 1), 0, 0)
    if final:
        t_lat = s - n_ctx_tiles * TM
        h_shape = jax.ShapeDtypeStruct((b, t_lat, d), F32)
        h_spec = pl.BlockSpec((1, TM, d), lambda i, j: (i, jnp.maximum(j - n_ctx_tiles, 0), 0))
    else:
        h_shape = jax.ShapeDtypeStruct((b, s, d), MXU_DTYPE)
        h_spec = pl.BlockSpec((1, TM, d), tok)
    return pl.pallas_call(
        functools.partial(_combine_kernel, rows=rows, final=final),
        grid=(b, nt),
        in_specs=[pl.BlockSpec((1, TM), lambda i, j: (0, i * nt + j), memory_space=pltpu.SMEM),
                  pl.BlockSpec((1, TM), lambda i, j: (0, jnp.minimum(i * nt + j + 1, b * nt - 1)),
                               memory_space=pltpu.SMEM),
                  pl.BlockSpec(memory_space=pl.ANY),
                  pl.BlockSpec((1, TM, d), tok),
                  pl.BlockSpec((TM, LANES), lambda i, j: (i * nt + j, 0)),
                  pl.BlockSpec((1, 1, SUB, d), msel),
                  pl.BlockSpec((1, 1, SUB, d), msel),
                  pl.BlockSpec((1, d), lambda i, j: (0, 0))],
        out_specs=[pl.BlockSpec((1, TM, d), tok), h_spec],
        out_shape=[jax.ShapeDtypeStruct((b, s, d), F32), h_shape],
        scratch_shapes=[pltpu.VMEM((2, TM * rows, LANES), F32), pltpu.SemaphoreType.DMA((2,))],
        compiler_params=_cparams(("arbitrary", "arbitrary")),
        name="moe_combine_final" if final else "moe_combine",
    )(pos, pos, ys, x, wcol, modsel, modsel_next, g_next.reshape(1, d))


def _grid_sincos(n_tokens, dim):
    rows = n_tokens // GRID_W
    row = np.broadcast_to(np.arange(rows)[:, None], (rows, GRID_W)).reshape(-1).astype(np.float32)
    col = np.broadcast_to(np.arange(GRID_W)[None, :], (rows, GRID_W)).reshape(-1).astype(np.float32)
    quarter = dim // 4
    freqs = np.exp(np.float32(-math.log(10000.0)) * np.arange(quarter, dtype=np.float32) / np.float32(quarter))
    ar = row[:, None] * freqs
    ac = col[:, None] * freqs
    return np.concatenate([np.sin(ar), np.cos(ar), np.sin(ac), np.cos(ac)], axis=-1).astype(np.float32)


def _dft_mats(n, scale, rows=None):
    k = np.arange(n) if rows is None else np.asarray(rows)
    ang = ((k[:, None] * np.arange(n)[None, :]) % n) * (2.0 * math.pi / n)
    return (np.cos(ang) * scale).astype(np.float32), (np.sin(ang) * scale).astype(np.float32)


def _split_cols(w, sizes):
    out, start = [], 0
    for sz in sizes:
        out.append(w[..., start:start + sz])
        start += sz
    return out


def _layer_weights(d, w_in_l, gla_a_w_l, gla_a_b_l, dt_bias_l, a_log_l, ssd_d_l):
    qk = GLA_HEADS * GLA_DK
    vv = GLA_HEADS * GLA_DV
    inner = SSD_GROUPS * HEADS_PER_GROUP * SSD_HEADDIM
    gn = SSD_GROUPS * SSD_STATE
    nh = SSD_GROUPS * HEADS_PER_GROUP
    sizes = (qk, qk, vv, vv, GLA_RANK, GLA_RANK, d, inner, inner + 2 * gn, nh, nh, d, d, d)
    (wq, wk, wv, wr, wlf, wlb, wu, wz, wxbc, wdf, wdb, wga, wgb, wgc) = _split_cols(w_in_l, sizes)
    w_main = jnp.concatenate([wq, wk, wv, wr, wu, wz, wxbc, wga, wgb, wgc], axis=1).astype(MXU_DTYPE)
    hp = HEADS_PER_GROUP
    pad = lambda w, width: jnp.pad(w, ((0, 0), (0, width - w.shape[1])))
    small = [pad(jnp.concatenate([wlf, wlb], axis=1), LANES)]
    for g in range(SSD_GROUPS):
        small.append(pad(jnp.concatenate([wdf[:, g * hp:(g + 1) * hp], wdb[:, g * hp:(g + 1) * hp]], axis=1), LANES))
    w_small = jnp.concatenate(small, axis=1).astype(MXU_DTYPE)
    aw_pad = jnp.zeros((2, LANES, qk), F32)
    aw_pad = aw_pad.at[0, :GLA_RANK].set(gla_a_w_l[0]).at[1, GLA_RANK:2 * GLA_RANK].set(gla_a_w_l[1])
    aw_pad = aw_pad.astype(MXU_DTYPE)
    ab = gla_a_b_l.reshape(2, 1, qk)
    grp = lambda v: v.reshape(SSD_GROUPS, hp)
    brow = jnp.concatenate([grp(dt_bias_l[0]), grp(dt_bias_l[1]), jnp.zeros((SSD_GROUPS, LANES - 2 * hp), F32)], axis=1)
    arow = jnp.concatenate([grp(a_log_l[0]), grp(a_log_l[1]), jnp.zeros((SSD_GROUPS, LANES - 2 * hp), F32)], axis=1)
    d_exp = jnp.repeat(ssd_d_l, SSD_HEADDIM).reshape(SSD_GROUPS, 1, hp * SSD_HEADDIM)
    return w_main, w_small, aw_pad, ab, brow.reshape(SSD_GROUPS, 1, LANES), arow.reshape(SSD_GROUPS, 1, LANES), d_exp


def kernel(x, c, ctx, c_ctx, ada_w, ada_b, norm1_g, norm2_g, w_in, gla_a_w, gla_a_b, gla_norm_g, gla_proj,
           fnet_proj, ssd_conv_w, ssd_conv_b, ssd_dt_bias, ssd_a_log, ssd_d, ssd_norm_g, ssd_proj, w_out,
           router_w, router_b, exp_w1, exp_w3, exp_w2, final_g):
    bsz, t_lat, d = x.shape
    n_ctx = ctx.shape[1]
    depth = ada_w.shape[0]
    s = n_ctx + t_lat
    n = bsz * s
    nct = n_ctx // TM
    ns = d // LANES

    xs0 = jnp.concatenate([ctx, x + _grid_sincos(t_lat, d)], axis=1)

    rows = 16
    c_all = jnp.concatenate([c, c_ctx[None], jnp.zeros((rows - bsz - 1, d), F32)], axis=0)
    mods = _ada(c_all, ada_w, ada_b)
    mods = mods.reshape(depth, rows, N_MOD, d)
    mods = jnp.pad(mods, ((0, 0), (0, 0), (0, SUB - N_MOD), (0, 0)))
    modsel = jnp.stack([jnp.broadcast_to(mods[:, bsz:bsz + 1], (depth, bsz, SUB, d)), mods[:, :bsz]], axis=2)

    width = d
    gd = width // FNET_GROUPS
    w_cs = jnp.asarray(np.concatenate(_dft_mats(gd, gd ** -0.5), axis=1), MXU_DTYPE)
    cos_c, sin_c = (jnp.asarray(m, MXU_DTYPE) for m in _dft_mats(n_ctx, n_ctx ** -0.5))
    n_win = t_lat // (2 * TM)
    win_rows = (np.arange(n_win)[:, None] * TM + np.arange(TM + SUB)[None, :]).reshape(-1)
    cos_w, sin_w = (jnp.asarray(m.reshape(n_win, TM + SUB, t_lat), MXU_DTYPE)
                    for m in _dft_mats(t_lat, t_lat ** -0.5, win_rows))
    jrev = jnp.asarray(np.arange(TM + SUB)[None, :] == TM - np.arange(TM)[:, None], MXU_DTYPE)

    rw_t = router_w.T
    rb = router_b.reshape(N_EXPERTS, 1)
    lo_tab = jnp.array([4 * (bk // N_PAIRS) + PAIR_LO[bk % N_PAIRS] for bk in range(N_BUCKETS)], I32)
    hi_tab = jnp.array([4 * (bk // N_PAIRS) + PAIR_HI[bk % N_PAIRS] for bk in range(N_BUCKETS)], I32)
    cap_rows = n + N_BUCKETS * TMX
    ntile = cap_rows // TMX

    xs_sorted = jnp.zeros((cap_rows * ns, LANES), F32)
    xcur = xs0
    h = _normmod(xcur, norm1_g[0], modsel[0], nct)
    out = None
    for l in range(depth):
        w_main, w_small, aw_pad, ab, brow, arow, d_exp = _layer_weights(
            d, w_in[l], gla_a_w[l], gla_a_b[l], ssd_dt_bias[l], ssd_a_log[l], ssd_d[l])
        p, ps = _inproj(h, w_main, w_small)
        xbc = _conv(p, ssd_conv_w[l], ssd_conv_b[l], n_ctx)
        ga = _gla(p, ps, aw_pad, ab, gla_norm_g[l], n_ctx)
        yz = _ssd(p, xbc, ps, brow, arow, d_exp, n_ctx)
        v = _chan_dft(p, w_cs, width)
        fmc = _pos_dft_ctx(v, cos_c, sin_c, width)
        fml = _pos_dft_lat(v, cos_w, sin_w, jrev, n_ctx, width)
        xmid, ht, ri, rf, wcol, cnt = _merge(
            ga, fmc, fml, yz, p, xcur, modsel[l], norm2_g[l].reshape(1, d), ssd_norm_g[l].reshape(1, -1),
            gla_proj[l].astype(MXU_DTYPE), fnet_proj[l].astype(MXU_DTYPE), ssd_proj[l].astype(MXU_DTYPE),
            w_out[l].astype(MXU_DTYPE), rw_t, rb, nct)

        counts = cnt[:N_BUCKETS, 0]
        tiles_per = (counts + TMX - 1) // TMX
        tile_end = jnp.cumsum(tiles_per)
        tile_start = tile_end - tiles_per
        bucket = ri[0]
        pos = (jnp.take(tile_start, bucket) * TMX + ri[1]).reshape(1, n)
        tile_bucket = jnp.minimum(jnp.searchsorted(tile_end, jnp.arange(ntile, dtype=I32), side="right"),
                                  N_BUCKETS - 1).astype(I32)
        tile_a = jnp.take(lo_tab, tile_bucket)
        tile_b = jnp.take(hi_tab, tile_bucket)
        nvalid = tile_end[-1:].astype(I32)

        xs_sorted = _dispatch(pos, ht, xs_sorted, ns)
        ys = _experts(tile_a, tile_b, nvalid, xs_sorted, exp_w1[l].astype(MXU_DTYPE), exp_w3[l].astype(MXU_DTYPE),
                      exp_w2[l].astype(MXU_DTYPE), d)
        final = l == depth - 1
        if final:
            xcur, out = _combine(pos, ys, xmid, wcol, modsel[l], modsel[l], final_g, nct, True)
        else:
            xcur, h = _combine(pos, ys, xmid, wcol, modsel[l], modsel[l + 1], norm1_g[l + 1], nct, False)
    return out
```

```python
import functools
import math

import jax
import jax.numpy as jnp
import numpy as np
from jax import lax
from jax.experimental import pallas as pl
from jax.experimental.pallas import tpu as pltpu

F32 = jnp.float32
BF16 = jnp.bfloat16
I32 = jnp.int32
MXU_DTYPE = BF16

EPS = 1e-6
LOG2E = 1.4426950408889634
GRID_W = 64
N_MOD = 6

GLA_HEADS = 4
GLA_DK = 128
GLA_DV = 256
GLA_RANK = 16
GLA_TAU = 16.0
GLA_CHUNK = 64
GLA_HPS = 2
GLA_BLOCK = 256

FNET_GROUPS = 4

SSD_HEADDIM = 64
SSD_GROUPS = 4
SSD_STATE = 128
SSD_CONV = 5
CONV_EDGE = 16
SSD_CHUNK = 128
HEADS_PER_GROUP = 8

N_EXPERTS = 16
N_EXPERT_GROUPS = 4
EXPERTS_PER_GROUP = 4
D_FF = 512
N_PAIRS = 6
N_BUCKETS = N_EXPERT_GROUPS * N_PAIRS
PAIR_LO = (0, 0, 0, 1, 1, 2)
PAIR_HI = (1, 2, 3, 2, 3, 3)

TM = 256
TMX = 256
TMD = 512
LANES = 128
SUB = 8
VMEM_LIMIT = 56 * 1024 * 1024

COL_Q, COL_K, COL_V, COL_R, COL_U, COL_Z, COL_X, COL_B, COL_C, COL_GA, COL_GB, COL_GC = (
    0, 512, 1024, 2048, 3072, 4096, 6144, 8192, 8704, 9216, 10240, 11264)
P_MAIN = 12288
P_SMALL = LANES * (1 + SSD_GROUPS)
IN_TN = 1024


def _cparams(sem, vmem=VMEM_LIMIT):
    return pltpu.CompilerParams(dimension_semantics=sem, vmem_limit_bytes=vmem)


def _mm(a, b):
    return jnp.dot(a.astype(MXU_DTYPE), b.astype(MXU_DTYPE), preferred_element_type=F32)


def _mm_nt(a, b):
    return lax.dot_general(a.astype(MXU_DTYPE), b.astype(MXU_DTYPE), (((1,), (1,)), ((), ())),
                           preferred_element_type=F32)


def _mm_tn(a, b):
    return lax.dot_general(a.astype(MXU_DTYPE), b.astype(MXU_DTYPE), (((0,), (0,)), ((), ())),
                           preferred_element_type=F32)


def _split3(x):
    hi = x.astype(MXU_DTYPE)
    r1 = x - hi.astype(F32)
    mid = r1.astype(MXU_DTYPE)
    lo = (r1 - mid.astype(F32)).astype(MXU_DTYPE)
    return hi, mid, lo


def _mm_exact_lhs(m01, x):
    hi = x.astype(MXU_DTYPE)
    lo = (x - hi.astype(F32)).astype(MXU_DTYPE)
    m = m01.astype(MXU_DTYPE)
    return jnp.dot(m, hi, preferred_element_type=F32) + jnp.dot(m, lo, preferred_element_type=F32)


def _sigmoid(x):
    return 1.0 / (1.0 + jnp.exp(-x))


def _silu(x):
    return x * _sigmoid(x)


def _softplus(x):
    return jnp.maximum(x, 0.0) + jnp.log(1.0 + jnp.exp(-jnp.abs(x)))


def _rms(x, g):
    return x * lax.rsqrt(jnp.mean(x * x, axis=-1, keepdims=True) + EPS) * g


def _modulate(x, g, shift, scale):
    return _rms(x, g) * (1.0 + scale) + shift


def _ada_kernel(c_ref, w_ref, b_ref, o_ref):
    o_ref[0] = _mm(_silu(c_ref[...]), w_ref[0]) + b_ref[0]


def _ada(c_all, ada_w, ada_b):
    depth, d, nd = ada_w.shape
    rows = c_all.shape[0]
    tn = 1024
    return pl.pallas_call(
        _ada_kernel,
        grid=(depth, nd // tn),
        in_specs=[pl.BlockSpec((rows, d), lambda l, j: (0, 0)),
                  pl.BlockSpec((1, d, tn), lambda l, j: (l, 0, j)),
                  pl.BlockSpec((1, 1, tn), lambda l, j: (l, 0, j))],
        out_specs=pl.BlockSpec((1, rows, tn), lambda l, j: (l, 0, j)),
        out_shape=jax.ShapeDtypeStruct((depth, rows, nd), F32),
        compiler_params=_cparams(("arbitrary", "arbitrary")),
        name="ada_mod",
    )(c_all, ada_w, ada_b.reshape(depth, 1, nd))


def _normmod_kernel(x_ref, g_ref, mod_ref, h_ref):
    mod = mod_ref[0, 0]
    h_ref[0] = _modulate(x_ref[0], g_ref[...], mod[0:1], mod[1:2]).astype(h_ref.dtype)


def _normmod(x, g, modsel, n_ctx_tiles):
    b, s, d = x.shape
    return pl.pallas_call(
        _normmod_kernel,
        grid=(b, s // TM),
        in_specs=[pl.BlockSpec((1, TM, d), lambda i, j: (i, j, 0)),
                  pl.BlockSpec((1, d), lambda i, j: (0, 0)),
                  pl.BlockSpec((1, 1, SUB, d), lambda i, j: (i, jnp.where(j < n_ctx_tiles, 0, 1), 0, 0))],
        out_specs=pl.BlockSpec((1, TM, d), lambda i, j: (i, j, 0)),
        out_shape=jax.ShapeDtypeStruct((b, s, d), MXU_DTYPE),
        compiler_params=_cparams(("arbitrary", "arbitrary")),
        name="norm_mod",
    )(x, g.reshape(1, d), modsel)


def _inproj_kernel(h_ref, w_ref, ws_ref, p_ref, ps_ref):
    h = h_ref[0]
    p_ref[0] = _mm(h, w_ref[...]).astype(p_ref.dtype)

    @pl.when(pl.program_id(1) == 0)
    def _():
        ps_ref[0] = _mm(h, ws_ref[...])


def _inproj(h, w_main, w_small):
    b, s, d = h.shape
    return pl.pallas_call(
        _inproj_kernel,
        grid=(b, P_MAIN // IN_TN),
        in_specs=[pl.BlockSpec((1, s, d), lambda i, j: (i, 0, 0)),
                  pl.BlockSpec((d, IN_TN), lambda i, j: (0, j)),
                  pl.BlockSpec((d, P_SMALL), lambda i, j: (0, 0))],
        out_specs=[pl.BlockSpec((1, s, IN_TN), lambda i, j: (i, 0, j)),
                   pl.BlockSpec((1, s, P_SMALL), lambda i, j: (i, 0, 0))],
        out_shape=[jax.ShapeDtypeStruct((b, s, P_MAIN), MXU_DTYPE),
                   jax.ShapeDtypeStruct((b, s, P_SMALL), F32)],
        compiler_params=_cparams(("arbitrary", "arbitrary")),
        name="in_proj",
    )(h, w_main, w_small)


def _conv_kernel(p_ref, w_ref, b_ref, o_ref, *, n_ctx):
    x = p_ref[0].astype(F32)
    s = x.shape[0]
    w = w_ref[...]
    bias = b_ref[...]
    half = SSD_CONV // 2

    def taps(xw, row0, masked):
        r = xw.shape[0]
        acc = jnp.zeros_like(xw) + bias
        if masked:
            row = row0 + lax.broadcasted_iota(I32, (r, 1), 0)
            seg_lo = jnp.where(row < n_ctx, 0, n_ctx)
            seg_hi = jnp.where(row < n_ctx, n_ctx, s)
        for k in range(SSD_CONV):
            dlt = k - half
            xs = xw if dlt == 0 else pltpu.roll(xw, shift=(-dlt) % r, axis=0)
            if masked:
                t2 = row + dlt
                ok = jnp.broadcast_to(jnp.where((t2 >= seg_lo) & (t2 < seg_hi), 1.0, 0.0), xw.shape)
                xs = jnp.where(ok > 0.5, xs, 0.0)
            acc = acc + w[k:k + 1] * xs
        return _silu(acc)

    o_ref[0] = taps(x, 0, False).astype(o_ref.dtype)
    eb, win = CONV_EDGE, 2 * CONV_EDGE
    for r0 in (0, n_ctx - eb, n_ctx, s - eb):
        a = min(max(r0 - SUB, 0), s - win)
        o_ref[0, r0:r0 + eb, :] = taps(x[a:a + win], a, True)[r0 - a:r0 - a + eb].astype(o_ref.dtype)


def _conv(p, conv_w, conv_b, n_ctx):
    b, s, _ = p.shape
    c = conv_w.shape[1]
    tc = 256
    return pl.pallas_call(
        functools.partial(_conv_kernel, n_ctx=n_ctx),
        grid=(b, c // tc),
        in_specs=[pl.BlockSpec((1, s, tc), lambda i, j: (i, 0, COL_X // tc + j)),
                  pl.BlockSpec((SSD_CONV, tc), lambda i, j: (0, j)),
                  pl.BlockSpec((1, tc), lambda i, j: (0, j))],
        out_specs=pl.BlockSpec((1, s, tc), lambda i, j: (i, 0, j)),
        out_shape=jax.ShapeDtypeStruct((b, s, c), MXU_DTYPE),
        compiler_params=_cparams(("arbitrary", "arbitrary")),
        name="ssd_conv",
    )(p, conv_w, conv_b.reshape(1, c))


def _gla_kernel(q_ref, k_ref, v_ref, r_ref, ps_ref, aw_ref, ab_ref, g_ref, o_ref,
                la_s, od_s, st_s, *, n_ctx):
    L = GLA_CHUNK
    R = GLA_BLOCK
    cpb = R // L
    dk, dv = GLA_DK, GLA_DV
    s = q_ref.shape[1]
    nb = s // R
    nbc = n_ctx // R
    lr = ps_ref[0]
    for d in range(2):
        pre = _mm(lr, aw_ref[d]) + ab_ref[d]
        la_s[d] = -_softplus(-pre) * (1.0 / GLA_TAU)

    ri = lax.broadcasted_iota(I32, (R, R), 0)
    ci = lax.broadcasted_iota(I32, (R, R), 1)
    same = (ri // L) == (ci // L)
    scale = dk ** -0.5
    st_s[...] = jnp.zeros_like(st_s)

    def per_chunk(x, idx):
        x3 = x.reshape(cpb, L, x.shape[-1])
        return jnp.broadcast_to(x3[:, idx:idx + 1, :], x3.shape).reshape(x.shape)

    def block(off, hh, reverse, st, keep, bcum_all):
        q = q_ref[0, pl.ds(off, R), hh * dk:(hh + 1) * dk].astype(F32) * scale
        k = k_ref[0, pl.ds(off, R), hh * dk:(hh + 1) * dk].astype(F32)
        v = v_ref[0, pl.ds(off, R), hh * dv:(hh + 1) * dv]
        bcum = bcum_all[:, hh * dk:(hh + 1) * dk]
        b_mid = per_chunk(bcum, L // 2 if reverse else L // 2 - 1)
        b_last = per_chunk(bcum, 0 if reverse else L - 1)
        sc = _mm_nt(q * jnp.exp(bcum - b_mid), k * jnp.exp(b_mid - bcum))
        o_intra = _mm(jnp.where(keep, sc, 0.0), v)
        qb = (q * jnp.exp(bcum)).astype(MXU_DTYPE)
        kd = (k * jnp.exp(b_last - bcum)).astype(MXU_DTYPE)
        dec = jnp.exp(b_last)
        o_inter = [None] * cpb
        for c in (range(cpb - 1, -1, -1) if reverse else range(cpb)):
            rows = slice(c * L, (c + 1) * L)
            o_inter[c] = _mm_nt(qb[rows], st)
            st = dec[c * L:c * L + 1] * st + _mm_tn(v[rows], kd[rows])
        return o_intra + jnp.concatenate(o_inter, axis=0), st

    @pl.loop(0, nb)
    def _(i):
        off_f = pl.multiple_of(i * R, R)
        jb = jnp.where(i < nbc, nbc - 1 - i, nb + nbc - 1 - i)
        off_b = pl.multiple_of(jb * R, R)
        offs = (off_f, off_b)
        keeps = (same & (ci <= ri), same & (ci >= ri))
        bcums = [_mm_exact_lhs(jnp.where(keeps[d], 1.0, 0.0), la_s[d, pl.ds(offs[d], R), :]) for d in range(2)]
        chains = [(d, hh) for hh in range(GLA_HPS) for d in range(2)]
        states = [st_s[d, hh] for d, hh in chains]
        res = [block(offs[d], hh, bool(d), st, keeps[d], bcums[d]) for (d, hh), st in zip(chains, states)]
        for (d, hh), (o, st) in zip(chains, res):
            od_s[d, pl.ds(off_b if d else off_f, R), hh * dv:(hh + 1) * dv] = o
            st_s[d, hh] = st

    g = g_ref[...]

    @pl.loop(0, nb)
    def _(c):
        off = pl.multiple_of(c * R, R)
        o = od_s[0, pl.ds(off, R), :] + od_s[1, pl.ds(off, R), :]
        r = r_ref[0, pl.ds(off, R), :].astype(F32)
        outs = [_rms(o[:, hh * dv:(hh + 1) * dv], g) for hh in range(GLA_HPS)]
        o_ref[0, pl.ds(off, R), :] = (jnp.concatenate(outs, axis=1) * _silu(r)).astype(o_ref.dtype)


def _gla(p, ps, aw_pad, ab, norm_g, n_ctx):
    b, s, _ = p.shape
    dk, dv = GLA_HPS * GLA_DK, GLA_HPS * GLA_DV
    return pl.pallas_call(
        functools.partial(_gla_kernel, n_ctx=n_ctx),
        grid=(b, GLA_HEADS // GLA_HPS),
        in_specs=[pl.BlockSpec((1, s, dk), lambda i, h: (i, 0, COL_Q // dk + h)),
                  pl.BlockSpec((1, s, dk), lambda i, h: (i, 0, COL_K // dk + h)),
                  pl.BlockSpec((1, s, dv), lambda i, h: (i, 0, COL_V // dv + h)),
                  pl.BlockSpec((1, s, dv), lambda i, h: (i, 0, COL_R // dv + h)),
                  pl.BlockSpec((1, s, LANES), lambda i, h: (i, 0, 0)),
                  pl.BlockSpec((2, LANES, dk), lambda i, h: (0, 0, h)),
                  pl.BlockSpec((2, 1, dk), lambda i, h: (0, 0, h)),
                  pl.BlockSpec((1, GLA_DV), lambda i, h: (0, 0))],
        out_specs=pl.BlockSpec((1, s, dv), lambda i, h: (i, 0, h)),
        out_shape=jax.ShapeDtypeStruct((b, s, GLA_HEADS * GLA_DV), MXU_DTYPE),
        scratch_shapes=[pltpu.VMEM((2, s, dk), F32), pltpu.VMEM((2, s, dv), F32),
                        pltpu.VMEM((2, GLA_HPS, GLA_DV, GLA_DK), F32)],
        compiler_params=_cparams(("arbitrary", "arbitrary")),
        name="gla",
    )(p, p, p, p, ps, aw_pad, ab, norm_g.reshape(1, GLA_DV))


def _ssd_kernel(z_ref, x_ref, bm_ref, cm_ref, ps_ref, brow_ref, alog_ref, d_ref, o_ref,
                dt_s, acs_s, rt_s, cb_s, y_s, st_s, *, n_ctx):
    L = SSD_CHUNK
    s = x_ref.shape[1]
    n = s // L
    nc = n_ctx // L
    hp = HEADS_PER_GROUP
    npair = hp // 2
    neg = -1e30
    dt_s[...] = _softplus(ps_ref[0] + brow_ref[0])
    a_row = -jnp.exp(alog_ref[0])

    ri = lax.broadcasted_iota(I32, (L, L), 0)
    ci = lax.broadcasted_iota(I32, (L, L), 1)
    lane = lax.broadcasted_iota(I32, (1, LANES), 1)
    low = lane < SSD_HEADDIM
    keeps = (ci <= ri, ci >= ri)
    er = lax.broadcasted_iota(I32, (LANES, hp * SSD_HEADDIM), 0)
    ec = lax.broadcasted_iota(I32, (LANES, hp * SSD_HEADDIM), 1) // SSD_HEADDIM
    esel = [jnp.where(er == d * hp + ec, 1.0, 0.0).astype(MXU_DTYPE) for d in range(2)]

    @pl.loop(0, n, unroll=2)
    def _(c):
        off = pl.multiple_of(c * L, L)
        dtc = dt_s[pl.ds(off, L), :]
        wc = dtc * a_row
        pre = _mm_exact_lhs(jnp.where(keeps[0], 1.0, 0.0), wc)
        suf = pre[L - 1:L] - pre + wc
        acs = jnp.where(lane < hp, pre, suf)
        acs_s[pl.ds(off, L), :] = acs * LOG2E
        rt_s[pl.ds(off, L), :] = ((acs - jnp.log(dtc)) * LOG2E).T
        cb_s[pl.ds(off, L), :] = _mm_nt(cm_ref[0, pl.ds(off, L), :], bm_ref[0, pl.ds(off, L), :])

    st_s[...] = jnp.zeros_like(st_s)

    def chunk(off, d, st):
        base = d * hp
        keep = keeps[d]
        i_last = 0 if d else L - 1
        xb = x_ref[0, pl.ds(off, L), :]
        bm = bm_ref[0, pl.ds(off, L), :]
        cm = cm_ref[0, pl.ds(off, L), :]
        dtc = dt_s[pl.ds(off, L), :]
        acs = acs_s[pl.ds(off, L), :]
        rt = rt_s[pl.ds(off, L), :]
        cb = cb_s[pl.ds(off, L), :]
        mine = (lane >= base) & (lane < base + hp)
        to_end = dtc * jnp.exp2(jnp.where(mine, acs[i_last:i_last + 1] - acs, 0.0))
        hi = to_end.astype(MXU_DTYPE)
        lo = (to_end - hi.astype(F32)).astype(MXU_DTYPE)
        to_end_e = (jnp.dot(hi, esel[d], preferred_element_type=F32)
                    + jnp.dot(lo, esel[d], preferred_element_type=F32))
        y_off = _mm(cm, st)
        ys, decs = [], []
        for pr in range(npair):
            a0 = acs[:, base + 2 * pr:base + 2 * pr + 1]
            a1 = acs[:, base + 2 * pr + 1:base + 2 * pr + 2]
            m2 = jnp.concatenate(
                [cb * jnp.exp2(jnp.where(keep, a0 - rt[base + 2 * pr:base + 2 * pr + 1, :], neg)),
                 cb * jnp.exp2(jnp.where(keep, a1 - rt[base + 2 * pr + 1:base + 2 * pr + 2, :], neg))], axis=1)
            xp = xb[:, pr * LANES:(pr + 1) * LANES]
            zero = jnp.zeros_like(xp)
            bd = jnp.concatenate([jnp.where(low, xp, zero), jnp.where(low, zero, xp)], axis=0)
            dec = jnp.exp2(jnp.where(low, a0, a1))
            ys.append(_mm(m2, bd) + y_off[:, pr * LANES:(pr + 1) * LANES] * dec)
            decs.append(dec[i_last:i_last + 1])
        st = jnp.concatenate(decs, axis=1) * st + _mm_tn(bm, xb.astype(F32) * to_end_e)
        return jnp.concatenate(ys, axis=1), st

    @pl.loop(0, n)
    def _(i):
        jb = jnp.where(i < nc, nc - 1 - i, n + nc - 1 - i)
        offs = (pl.multiple_of(i * L, L), pl.multiple_of(jb * L, L))
        states = [st_s[d] for d in range(2)]
        res = [chunk(offs[d], d, states[d]) for d in range(2)]
        for d, (y, st) in enumerate(res):
            y_s[d, pl.ds(offs[d], L), :] = y
            st_s[d] = st

    dsk = d_ref[0]

    @pl.loop(0, n)
    def _(c):
        off = pl.multiple_of(c * L, L)
        y = y_s[0, pl.ds(off, L), :] + y_s[1, pl.ds(off, L), :] + dsk * x_ref[0, pl.ds(off, L), :].astype(F32)
        z = z_ref[0, pl.ds(off, L), :].astype(F32)
        o_ref[0, pl.ds(off, L), :] = (y * _silu(z)).astype(o_ref.dtype)


def _ssd(p, xbc, ps, brow, alog_row, d_exp, n_ctx):
    b, s, _ = p.shape
    gw = HEADS_PER_GROUP * SSD_HEADDIM
    inner = SSD_GROUPS * gw
    ns = SSD_STATE
    return pl.pallas_call(
        functools.partial(_ssd_kernel, n_ctx=n_ctx),
        grid=(b, SSD_GROUPS),
        in_specs=[pl.BlockSpec((1, s, gw), lambda i, g: (i, 0, COL_Z // gw + g)),
                  pl.BlockSpec((1, s, gw), lambda i, g: (i, 0, g)),
                  pl.BlockSpec((1, s, ns), lambda i, g: (i, 0, inner // ns + g)),
                  pl.BlockSpec((1, s, ns), lambda i, g: (i, 0, inner // ns + SSD_GROUPS + g)),
                  pl.BlockSpec((1, s, LANES), lambda i, g: (i, 0, 1 + g)),
                  pl.BlockSpec((1, 1, LANES), lambda i, g: (g, 0, 0)),
                  pl.BlockSpec((1, 1, LANES), lambda i, g: (g, 0, 0)),
                  pl.BlockSpec((1, 1, gw), lambda i, g: (g, 0, 0))],
        out_specs=pl.BlockSpec((1, s, gw), lambda i, g: (i, 0, g)),
        out_shape=jax.ShapeDtypeStruct((b, s, inner), MXU_DTYPE),
        scratch_shapes=[pltpu.VMEM((s, LANES), F32), pltpu.VMEM((s, LANES), F32),
                        pltpu.VMEM((s, LANES), F32), pltpu.VMEM((s, LANES), F32),
                        pltpu.VMEM((2, s, gw), F32), pltpu.VMEM((2, ns, gw), F32)],
        compiler_params=_cparams(("arbitrary", "arbitrary")),
        name="ssd",
    )(p, xbc, xbc, xbc, ps, brow, alog_row, d_exp)


def _chan_dft_kernel(u_ref, w_ref, o_ref):
    u = u_ref[0]
    gd = w_ref.shape[0]
    w = w_ref[...]
    cs, sn = [], []
    for g in range(FNET_GROUPS):
        r = _mm(u[:, g * gd:(g + 1) * gd], w)
        cs.append(r[:, :gd])
        sn.append(r[:, gd:])
    o_ref[0] = jnp.concatenate(cs + sn, axis=1).astype(o_ref.dtype)


def _chan_dft(p, w_cs, width):
    b, s, _ = p.shape
    gd = width // FNET_GROUPS
    return pl.pallas_call(
        _chan_dft_kernel,
        grid=(b, s // TM),
        in_specs=[pl.BlockSpec((1, TM, width), lambda i, j: (i, j, COL_U // width)),
                  pl.BlockSpec((gd, 2 * gd), lambda i, j: (0, 0))],
        out_specs=pl.BlockSpec((1, TM, 2 * width), lambda i, j: (i, j, 0)),
        out_shape=jax.ShapeDtypeStruct((b, s, 2 * width), MXU_DTYPE),
        compiler_params=_cparams(("arbitrary", "arbitrary")),
        name="fnet_chan",
    )(p, w_cs)


def _pos_dft_ctx_kernel(c_ref, s_ref, uc_ref, us_ref, o_ref):
    o_ref[0] = (_mm(c_ref[...], uc_ref[0]) - _mm(s_ref[...], us_ref[0])).astype(o_ref.dtype)


def _pos_dft_ctx(v, cos_c, sin_c, width):
    b = v.shape[0]
    n_ctx = cos_c.shape[0]
    return pl.pallas_call(
        _pos_dft_ctx_kernel,
        grid=(b, n_ctx // TM),
        in_specs=[pl.BlockSpec((TM, n_ctx), lambda i, j: (j, 0)),
                  pl.BlockSpec((TM, n_ctx), lambda i, j: (j, 0)),
                  pl.BlockSpec((1, n_ctx, width), lambda i, j: (i, 0, 0)),
                  pl.BlockSpec((1, n_ctx, width), lambda i, j: (i, 0, 1))],
        out_specs=pl.BlockSpec((1, TM, width), lambda i, j: (i, j, 0)),
        out_shape=jax.ShapeDtypeStruct((b, n_ctx, width), MXU_DTYPE),
        compiler_params=_cparams(("arbitrary", "arbitrary")),
        name="fnet_pos_ctx",
    )(cos_c, sin_c, v, v)


def _pos_dft_lat_kernel(cw_ref, sw_ref, jr_ref, uc_ref, us_ref, o_ref, *, n_ctx):
    a = _mm(cw_ref[0], uc_ref[0, n_ctx:, :])
    bm = _mm(sw_ref[0], us_ref[0, n_ctx:, :])
    o_ref[0, :TM, :] = (a - bm)[:TM].astype(o_ref.dtype)
    o_ref[0, TM:, :] = _mm(jr_ref[...], a + bm).astype(o_ref.dtype)


def _pos_dft_lat(v, cos_w, sin_w, jrev, n_ctx, width):
    b, s, _ = v.shape
    nh, wr, t = cos_w.shape
    return pl.pallas_call(
        functools.partial(_pos_dft_lat_kernel, n_ctx=n_ctx),
        grid=(b, nh),
        in_specs=[pl.BlockSpec((1, wr, t), lambda i, j: (j, 0, 0)),
                  pl.BlockSpec((1, wr, t), lambda i, j: (j, 0, 0)),
                  pl.BlockSpec((TM, wr), lambda i, j: (0, 0)),
                  pl.BlockSpec((1, s, width), lambda i, j: (i, 0, 0)),
                  pl.BlockSpec((1, s, width), lambda i, j: (i, 0, 1))],
        out_specs=pl.BlockSpec((1, 2 * TM, width), lambda i, j: (i, j, 0)),
        out_shape=jax.ShapeDtypeStruct((b, t, width), MXU_DTYPE),
        compiler_params=_cparams(("arbitrary", "arbitrary")),
        name="fnet_pos",
    )(cos_w, sin_w, jrev, v, v)


def _merge_kernel(ga_ref, fmc_ref, fml_ref, yz_ref, g1_ref, g2_ref, g3_ref, x_ref, mod_ref, n2g_ref, sg_ref,
                  wa_ref, wb_ref, wc_ref, wo_ref, rw_ref, rb_ref,
                  xo_ref, ht_ref, ri_ref, rf_ref, wcol_ref, cnt_ref, carry_s, *, n_ctx_tiles):
    first = (pl.program_id(0) == 0) & (pl.program_id(1) == 0)

    @pl.when(first)
    def _():
        carry_s[...] = jnp.zeros_like(carry_s)

    mod = mod_ref[0, 0]
    ya = _mm(ga_ref[0], wa_ref[...])
    fm = jnp.where(pl.program_id(1) < n_ctx_tiles, fmc_ref[0], fml_ref[0])
    yb = _mm(fm, wb_ref[...])
    yz = yz_ref[0].astype(F32)
    yc = _mm(_rms(yz, sg_ref[...]), wc_ref[...])
    y = (_sigmoid(g1_ref[0].astype(F32)) * ya + _sigmoid(g2_ref[0].astype(F32)) * yb
         + _sigmoid(g3_ref[0].astype(F32)) * yc)
    xn = x_ref[0] + mod[2:3] * _mm(y, wo_ref[...])
    xo_ref[0] = xn
    h2 = _modulate(xn, n2g_ref[...], mod[3:4], mod[4:5])
    d = h2.shape[1]
    tm = h2.shape[0]
    for sidx in range(d // LANES):
        ht_ref[pl.ds(sidx, tm, stride=d // LANES), :] = h2[:, sidx * LANES:(sidx + 1) * LANES]

    hh, hm, _hl = _split3(h2)
    rh, rm, _rl = _split3(rw_ref[...])
    nt = (((1,), (1,)), ((), ()))
    lg = (lax.dot_general(rh, hh, nt, preferred_element_type=F32)
          + lax.dot_general(rh, hm, nt, preferred_element_type=F32)
          + lax.dot_general(rm, hh, nt, preferred_element_type=F32))
    sc = _sigmoid(lg)
    sel = sc + rb_ref[...]
    srow = [sel[e:e + 1] for e in range(N_EXPERTS)]
    urow = [sc[e:e + 1] for e in range(N_EXPERTS)]
    epg = EXPERTS_PER_GROUP

    best_v = best_pair = best_g = best_slo = best_shi = None
    for g in range(N_EXPERT_GROUPS):
        a = srow[g * epg:(g + 1) * epg]
        u = urow[g * epg:(g + 1) * epg]
        gs = None
        for i in range(epg):
            for j in range(i + 1, epg):
                ps_ = a[i] + a[j]
                gs = ps_ if gs is None else jnp.maximum(gs, ps_)
        m1, i1 = a[0], jnp.zeros_like(a[0], dtype=I32)
        for i in range(1, epg):
            up = a[i] > m1
            i1 = jnp.where(up, i, i1)
            m1 = jnp.where(up, a[i], m1)
        m2 = jnp.full_like(a[0], -jnp.inf)
        i2 = jnp.full_like(i1, -1)
        for i in range(epg):
            cand = (i1 != i) & (a[i] > m2)
            i2 = jnp.where(cand, i, i2)
            m2 = jnp.where(cand, a[i], m2)
        lo = jnp.minimum(i1, i2)
        hi = jnp.maximum(i1, i2)
        pair = jnp.where(lo == 0, 0, jnp.where(lo == 1, 3, 5)) + (hi - lo - 1)
        s_lo = jnp.where(lo == 0, u[0], jnp.where(lo == 1, u[1], u[2]))
        s_hi = jnp.where(hi == 1, u[1], jnp.where(hi == 2, u[2], u[3]))
        if g == 0:
            best_v, best_pair, best_g, best_slo, best_shi = gs, pair, jnp.zeros_like(pair), s_lo, s_hi
        else:
            up = gs > best_v
            best_v = jnp.where(up, gs, best_v)
            best_pair = jnp.where(up, pair, best_pair)
            best_g = jnp.where(up, g, best_g)
            best_slo = jnp.where(up, s_lo, best_slo)
            best_shi = jnp.where(up, s_hi, best_shi)
    bucket = best_g * N_PAIRS + best_pair
    tot = best_slo + best_shi
    w_lo = best_slo / tot
    w_hi = best_shi / tot

    nb = carry_s.shape[0]
    oh = (lax.broadcasted_iota(I32, (nb, tm), 0) == bucket).astype(F32)
    tri = (lax.broadcasted_iota(I32, (tm, tm), 0) <= lax.broadcasted_iota(I32, (tm, tm), 1))
    csum = _mm(oh, jnp.where(tri, 1.0, 0.0))
    carry = carry_s[...]
    rank = jnp.sum(oh * (csum - 1.0 + carry), axis=0, keepdims=True)
    carry = carry + jnp.sum(oh, axis=1, keepdims=True)
    carry_s[...] = carry
    cnt_ref[...] = jnp.broadcast_to(carry, cnt_ref.shape).astype(I32)

    zi = jnp.zeros((SUB - 2, tm), I32)
    ri_ref[...] = jnp.concatenate([bucket, rank.astype(I32), zi], axis=0)
    wrows = jnp.concatenate([w_lo, w_hi, jnp.zeros((SUB - 2, tm), F32)], axis=0)
    rf_ref[...] = wrows
    eye = (lax.broadcasted_iota(I32, (SUB, LANES), 0) == lax.broadcasted_iota(I32, (SUB, LANES), 1))
    eye = jnp.where(eye, 1.0, 0.0).astype(MXU_DTYPE)
    tn = (((0,), (0,)), ((), ()))
    p1, p2, p3 = _split3(wrows)
    wcol_ref[...] = (lax.dot_general(p1, eye, tn, preferred_element_type=F32)
                     + lax.dot_general(p2, eye, tn, preferred_element_type=F32)
                     + lax.dot_general(p3, eye, tn, preferred_element_type=F32))


def _merge(ga, fmc, fml, yz, p, x, modsel, n2g, ssd_g, w_a, w_b, w_c, w_o, rw_t, rb, n_ctx_tiles):
    b, s, d = x.shape
    nt = s // TM
    n = b * s
    di = yz.shape[2]
    nbp = 32
    tok = lambda i, j: (i, j, 0)
    flat = lambda i, j: (0, i * nt + j)
    const = lambda i, j: (0, 0)
    nct = n_ctx_tiles
    nh = (nt - nct) // 2

    def fml_idx(i, j):
        m = jnp.maximum(j - nct, 0)
        return (i, jnp.where(m < nh, 2 * m, 2 * (2 * nh - 1 - m) + 1), 0)

    return pl.pallas_call(
        functools.partial(_merge_kernel, n_ctx_tiles=nct),
        grid=(b, nt),
        in_specs=[pl.BlockSpec((1, TM, d), tok),
                  pl.BlockSpec((1, TM, d), lambda i, j: (i, jnp.minimum(j, nct - 1), 0)),
                  pl.BlockSpec((1, TM, d), fml_idx),
                  pl.BlockSpec((1, TM, di), tok),
                  pl.BlockSpec((1, TM, d), lambda i, j: (i, j, COL_GA // d)),
                  pl.BlockSpec((1, TM, d), lambda i, j: (i, j, COL_GB // d)),
                  pl.BlockSpec((1, TM, d), lambda i, j: (i, j, COL_GC // d)),
                  pl.BlockSpec((1, TM, d), tok),
                  pl.BlockSpec((1, 1, SUB, d), lambda i, j: (i, jnp.where(j < n_ctx_tiles, 0, 1), 0, 0)),
                  pl.BlockSpec((1, d), const),
                  pl.BlockSpec((1, di), const),
                  pl.BlockSpec((d, d), const),
                  pl.BlockSpec((d, d), const),
                  pl.BlockSpec((di, d), const),
                  pl.BlockSpec((d, d), const),
                  pl.BlockSpec((N_EXPERTS, d), const),
                  pl.BlockSpec((N_EXPERTS, 1), const)],
        out_specs=[pl.BlockSpec((1, TM, d), tok),
                   pl.BlockSpec((TM * (d // LANES), LANES), lambda i, j: (i * nt + j, 0)),
                   pl.BlockSpec((SUB, TM), flat),
                   pl.BlockSpec((SUB, TM), flat),
                   pl.BlockSpec((TM, LANES), lambda i, j: (i * nt + j, 0)),
                   pl.BlockSpec((nbp, LANES), const)],
        out_shape=[jax.ShapeDtypeStruct((b, s, d), F32),
                   jax.ShapeDtypeStruct((n * (d // LANES), LANES), F32),
                   jax.ShapeDtypeStruct((SUB, n), I32),
                   jax.ShapeDtypeStruct((SUB, n), F32),
                   jax.ShapeDtypeStruct((n, LANES), F32),
                   jax.ShapeDtypeStruct((nbp, LANES), I32)],
        scratch_shapes=[pltpu.VMEM((nbp, 1), F32)],
        compiler_params=_cparams(("arbitrary", "arbitrary")),
        name="merge_route",
    )(ga, fmc, fml, yz, p, p, p, x, modsel, n2g, ssd_g, w_a, w_b, w_c, w_o, rw_t, rb)


def _row_copy(src, dst, src_row, dst_row, rows, sem):
    def at(ref, r):
        start = r * rows if isinstance(r, int) else pl.multiple_of(r * rows, rows)
        return ref.at[pl.ds(start, rows)]
    return pltpu.make_async_copy(at(src, src_row), at(dst, dst_row), sem)


def _dispatch_kernel(pos_ref, ht_ref, xs_in_ref, xs_ref, sem, *, rows):
    del xs_in_ref
    copies = [_row_copy(ht_ref, xs_ref, t, pos_ref[0, t], rows, sem) for t in range(TMD)]
    for cp in copies:
        cp.start()
    for cp in copies:
        cp.wait()


def _dispatch(pos, ht, xs_init, rows):
    n = pos.shape[1]
    return pl.pallas_call(
        functools.partial(_dispatch_kernel, rows=rows),
        grid=(n // TMD,),
        in_specs=[pl.BlockSpec((1, TMD), lambda i: (0, i), memory_space=pltpu.SMEM),
                  pl.BlockSpec((TMD * rows, LANES), lambda i: (i, 0)),
                  pl.BlockSpec(memory_space=pl.ANY)],
        out_specs=pl.BlockSpec(memory_space=pl.ANY),
        out_shape=jax.ShapeDtypeStruct(xs_init.shape, xs_init.dtype),
        input_output_aliases={2: 0},
        scratch_shapes=[pltpu.SemaphoreType.DMA(())],
        compiler_params=_cparams(("arbitrary",)),
        name="moe_dispatch",
    )(pos, ht, xs_init)


def _expert_kernel(ta_ref, tb_ref, nv_ref, xs_ref, w1a_ref, w3a_ref, w2a_ref, w1b_ref, w3b_ref, w2b_ref, ys_ref):
    del ta_ref, tb_ref

    @pl.when(pl.program_id(0) >= nv_ref[0])
    def _():
        ys_ref[...] = jnp.zeros_like(ys_ref)

    @pl.when(pl.program_id(0) < nv_ref[0])
    def _():
        ns = xs_ref.shape[0] // TMX
        x = jnp.concatenate([xs_ref[pl.ds(sidx, TMX, stride=ns), :] for sidx in range(ns)], axis=1)
        x = x.astype(MXU_DTYPE)
        for kk, (w1_ref, w3_ref, w2_ref) in enumerate(((w1a_ref, w3a_ref, w2a_ref), (w1b_ref, w3b_ref, w2b_ref))):
            he = _silu(_mm(x, w1_ref[0, 0])) * _mm(x, w3_ref[0, 0])
            y = _mm(he, w2_ref[0, 0])
            for sidx in range(ns):
                ys_ref[pl.ds(kk * ns + sidx, TMX, stride=2 * ns), :] = y[:, sidx * LANES:(sidx + 1) * LANES]


def _experts(tile_a, tile_b, nvalid, xs, w1, w3, w2, layer, d):
    ns = d // LANES
    ntile = xs.shape[0] // (TMX * ns)
    last = lambda i, nv: jnp.minimum(i, nv[0] - 1)
    up = lambda t: pl.BlockSpec((1, 1, d, D_FF), lambda i, ta, tb, nv: (layer, (ta, tb)[t][last(i, nv)], 0, 0))
    down = lambda t: pl.BlockSpec((1, 1, D_FF, d), lambda i, ta, tb, nv: (layer, (ta, tb)[t][last(i, nv)], 0, 0))
    grid_spec = pltpu.PrefetchScalarGridSpec(
        num_scalar_prefetch=3,
        grid=(ntile,),
        in_specs=[pl.BlockSpec((TMX * ns, LANES), lambda i, ta, tb, nv: (last(i, nv), 0)),
                  up(0), up(0), down(0), up(1), up(1), down(1)],
        out_specs=pl.BlockSpec((TMX * 2 * ns, LANES), lambda i, ta, tb, nv: (i, 0)),
    )
    return pl.pallas_call(
        _expert_kernel,
        grid_spec=grid_spec,
        out_shape=jax.ShapeDtypeStruct((ntile * TMX * 2 * ns, LANES), F32),
        compiler_params=_cparams(("arbitrary",)),
        name="moe_experts",
    )(tile_a, tile_b, nvalid, xs, w1, w3, w2, w1, w3, w2)


def _combine_kernel(pos_ref, posn_ref, ys_ref, x_ref, wcol_ref, mod_ref, modn_ref, g_ref, xo_ref, ho_ref, buf, sem,
                    *, rows, final):
    step = pl.program_id(0) * pl.num_programs(1) + pl.program_id(1)
    last = pl.num_programs(0) * pl.num_programs(1) - 1
    slot = step % 2

    def gathers(p_ref, sl):
        return [_row_copy(ys_ref, buf.at[sl], p_ref[0, t], t, rows, sem.at[sl]) for t in range(TM)]

    @pl.when(step == 0)
    def _():
        for cp in gathers(pos_ref, 0):
            cp.start()

    @pl.when(step < last)
    def _():
        for cp in gathers(posn_ref, 1 - slot):
            cp.start()

    for cp in gathers(pos_ref, slot):
        cp.wait()

    ns = rows // 2
    ya = jnp.concatenate([buf[slot, pl.ds(sidx, TM, stride=rows), :] for sidx in range(ns)], axis=1)
    yb = jnp.concatenate([buf[slot, pl.ds(ns + sidx, TM, stride=rows), :] for sidx in range(ns)], axis=1)
    wc = wcol_ref[...]
    y = wc[:, 0:1] * ya + wc[:, 1:2] * yb
    mod = mod_ref[0, 0]
    xn = x_ref[0] + mod[5:6] * y
    xo_ref[0] = xn
    if final:
        ho_ref[0] = _rms(xn, g_ref[...]).astype(ho_ref.dtype)
    else:
        modn = modn_ref[0, 0]
        ho_ref[0] = _modulate(xn, g_ref[...], modn[0:1], modn[1:2]).astype(ho_ref.dtype)


def _combine(pos, ys, x, wcol, modsel, modsel_next, g_next, n_ctx_tiles, final):
    b, s, d = x.shape
    nt = s // TM
    rows = 2 * (d // LANES)
    tok = lambda i, j: (i, j, 0)
    msel = lambda i, j: (i, jnp.where(j < n_ctx_tiles, 0, 1), 0, 0)
    if final:
        t_lat = s - n_ctx_tiles * TM
        h_shape = jax.ShapeDtypeStruct((b, t_lat, d), F32)
        h_spec = pl.BlockSpec((1, TM, d), lambda i, j: (i, jnp.maximum(j - n_ctx_tiles, 0), 0))
    else:
        h_shape = jax.ShapeDtypeStruct((b, s, d), MXU_DTYPE)
        h_spec = pl.BlockSpec((1, TM, d), tok)
    return pl.pallas_call(
        functools.partial(_combine_kernel, rows=rows, final=final),
        grid=(b, nt),
        in_specs=[pl.BlockSpec((1, TM), lambda i, j: (0, i * nt + j), memory_space=pltpu.SMEM),
                  pl.BlockSpec((1, TM), lambda i, j: (0, jnp.minimum(i * nt + j + 1, b * nt - 1)),
                               memory_space=pltpu.SMEM),
                  pl.BlockSpec(memory_space=pl.ANY),
                  pl.BlockSpec((1, TM, d), tok),
                  pl.BlockSpec((TM, LANES), lambda i, j: (i * nt + j, 0)),
                  pl.BlockSpec((1, 1, SUB, d), msel),
                  pl.BlockSpec((1, 1, SUB, d), msel),
                  pl.BlockSpec((1, d), lambda i, j: (0, 0))],
        out_specs=[pl.BlockSpec((1, TM, d), tok), h_spec],
        out_shape=[jax.ShapeDtypeStruct((b, s, d), F32), h_shape],
        scratch_shapes=[pltpu.VMEM((2, TM * rows, LANES), F32), pltpu.SemaphoreType.DMA((2,))],
        compiler_params=_cparams(("arbitrary", "arbitrary")),
        name="moe_combine_final" if final else "moe_combine",
    )(pos, pos, ys, x, wcol, modsel, modsel_next, g_next.reshape(1, d))


def _grid_sincos(n_tokens, dim):
    rows = n_tokens // GRID_W
    row = np.broadcast_to(np.arange(rows)[:, None], (rows, GRID_W)).reshape(-1).astype(np.float32)
    col = np.broadcast_to(np.arange(GRID_W)[None, :], (rows, GRID_W)).reshape(-1).astype(np.float32)
    quarter = dim // 4
    freqs = np.exp(np.float32(-math.log(10000.0)) * np.arange(quarter, dtype=np.float32) / np.float32(quarter))
    ar = row[:, None] * freqs
    ac = col[:, None] * freqs
    return np.concatenate([np.sin(ar), np.cos(ar), np.sin(ac), np.cos(ac)], axis=-1).astype(np.float32)


def _dft_mats(n, scale, rows=None):
    k = np.arange(n) if rows is None else np.asarray(rows)
    ang = ((k[:, None] * np.arange(n)[None, :]) % n) * (2.0 * math.pi / n)
    return (np.cos(ang) * scale).astype(np.float32), (np.sin(ang) * scale).astype(np.float32)


def _split_cols(w, sizes):
    out, start = [], 0
    for sz in sizes:
        out.append(w[..., start:start + sz])
        start += sz
    return out


def _layer_weights(d, w_in_l, gla_a_w_l, gla_a_b_l, dt_bias_l, a_log_l, ssd_d_l):
    qk = GLA_HEADS * GLA_DK
    vv = GLA_HEADS * GLA_DV
    inner = SSD_GROUPS * HEADS_PER_GROUP * SSD_HEADDIM
    gn = SSD_GROUPS * SSD_STATE
    nh = SSD_GROUPS * HEADS_PER_GROUP
    sizes = (qk, qk, vv, vv, GLA_RANK, GLA_RANK, d, inner, inner + 2 * gn, nh, nh, d, d, d)
    (wq, wk, wv, wr, wlf, wlb, wu, wz, wxbc, wdf, wdb, wga, wgb, wgc) = _split_cols(w_in_l, sizes)
    w_main = jnp.concatenate([wq, wk, wv, wr, wu, wz, wxbc, wga, wgb, wgc], axis=1).astype(MXU_DTYPE)
    hp = HEADS_PER_GROUP
    pad = lambda w, width: jnp.pad(w, ((0, 0), (0, width - w.shape[1])))
    small = [pad(jnp.concatenate([wlf, wlb], axis=1), LANES)]
    for g in range(SSD_GROUPS):
        small.append(pad(jnp.concatenate([wdf[:, g * hp:(g + 1) * hp], wdb[:, g * hp:(g + 1) * hp]], axis=1), LANES))
    w_small = jnp.concatenate(small, axis=1).astype(MXU_DTYPE)
    aw_pad = jnp.zeros((2, LANES, qk), F32)
    aw_pad = aw_pad.at[0, :GLA_RANK].set(gla_a_w_l[0]).at[1, GLA_RANK:2 * GLA_RANK].set(gla_a_w_l[1])
    aw_pad = aw_pad.astype(MXU_DTYPE)
    ab = gla_a_b_l.reshape(2, 1, qk)
    grp = lambda v: v.reshape(SSD_GROUPS, hp)
    brow = jnp.concatenate([grp(dt_bias_l[0]), grp(dt_bias_l[1]), jnp.zeros((SSD_GROUPS, LANES - 2 * hp), F32)], axis=1)
    arow = jnp.concatenate([grp(a_log_l[0]), grp(a_log_l[1]), jnp.zeros((SSD_GROUPS, LANES - 2 * hp), F32)], axis=1)
    d_exp = jnp.repeat(ssd_d_l, SSD_HEADDIM).reshape(SSD_GROUPS, 1, hp * SSD_HEADDIM)
    return w_main, w_small, aw_pad, ab, brow.reshape(SSD_GROUPS, 1, LANES), arow.reshape(SSD_GROUPS, 1, LANES), d_exp


def kernel(x, c, ctx, c_ctx, ada_w, ada_b, norm1_g, norm2_g, w_in, gla_a_w, gla_a_b, gla_norm_g, gla_proj,
           fnet_proj, ssd_conv_w, ssd_conv_b, ssd_dt_bias, ssd_a_log, ssd_d, ssd_norm_g, ssd_proj, w_out,
           router_w, router_b, exp_w1, exp_w3, exp_w2, final_g):
    bsz, t_lat, d = x.shape
    n_ctx = ctx.shape[1]
    depth = ada_w.shape[0]
    s = n_ctx + t_lat
    n = bsz * s
    nct = n_ctx // TM
    ns = d // LANES

    xs0 = jnp.concatenate([ctx, x + _grid_sincos(t_lat, d)], axis=1)

    rows = 16
    c_all = jnp.concatenate([c, c_ctx[None], jnp.zeros((rows - bsz - 1, d), F32)], axis=0)
    mods = _ada(c_all, ada_w, ada_b)
    mods = mods.reshape(depth, rows, N_MOD, d)
    mods = jnp.pad(mods, ((0, 0), (0, 0), (0, SUB - N_MOD), (0, 0)))
    modsel = jnp.stack([jnp.broadcast_to(mods[:, bsz:bsz + 1], (depth, bsz, SUB, d)), mods[:, :bsz]], axis=2)

    width = d
    gd = width // FNET_GROUPS
    w_cs = jnp.asarray(np.concatenate(_dft_mats(gd, gd ** -0.5), axis=1), MXU_DTYPE)
    cos_c, sin_c = (jnp.asarray(m, MXU_DTYPE) for m in _dft_mats(n_ctx, n_ctx ** -0.5))
    n_win = t_lat // (2 * TM)
    win_rows = (np.arange(n_win)[:, None] * TM + np.arange(TM + SUB)[None, :]).reshape(-1)
    cos_w, sin_w = (jnp.asarray(m.reshape(n_win, TM + SUB, t_lat), MXU_DTYPE)
                    for m in _dft_mats(t_lat, t_lat ** -0.5, win_rows))
    jrev = jnp.asarray(np.arange(TM + SUB)[None, :] == TM - np.arange(TM)[:, None], MXU_DTYPE)

    rw_t = router_w.T
    rb = router_b.reshape(N_EXPERTS, 1)
    lo_tab = jnp.array([4 * (bk // N_PAIRS) + PAIR_LO[bk % N_PAIRS] for bk in range(N_BUCKETS)], I32)
    hi_tab = jnp.array([4 * (bk // N_PAIRS) + PAIR_HI[bk % N_PAIRS] for bk in range(N_BUCKETS)], I32)
    cap_rows = n + N_BUCKETS * TMX
    ntile = cap_rows // TMX

    w1_all, w3_all, w2_all = (w.astype(MXU_DTYPE) for w in (exp_w1, exp_w3, exp_w2))
    xs_sorted = jnp.zeros((cap_rows * ns, LANES), F32)
    xcur = xs0
    h = _normmod(xcur, norm1_g[0], modsel[0], nct)
    out = None
    for l in range(depth):
        w_main, w_small, aw_pad, ab, brow, arow, d_exp = _layer_weights(
            d, w_in[l], gla_a_w[l], gla_a_b[l], ssd_dt_bias[l], ssd_a_log[l], ssd_d[l])
        p, ps = _inproj(h, w_main, w_small)
        xbc = _conv(p, ssd_conv_w[l], ssd_conv_b[l], n_ctx)
        ga = _gla(p, ps, aw_pad, ab, gla_norm_g[l], n_ctx)
        yz = _ssd(p, xbc, ps, brow, arow, d_exp, n_ctx)
        v = _chan_dft(p, w_cs, width)
        fmc = _pos_dft_ctx(v, cos_c, sin_c, width)
        fml = _pos_dft_lat(v, cos_w, sin_w, jrev, n_ctx, width)
        xmid, ht, ri, rf, wcol, cnt = _merge(
            ga, fmc, fml, yz, p, xcur, modsel[l], norm2_g[l].reshape(1, d), ssd_norm_g[l].reshape(1, -1),
            gla_proj[l].astype(MXU_DTYPE), fnet_proj[l].astype(MXU_DTYPE), ssd_proj[l].astype(MXU_DTYPE),
            w_out[l].astype(MXU_DTYPE), rw_t, rb, nct)

        counts = cnt[:N_BUCKETS, 0]
        tiles_per = (counts + TMX - 1) // TMX
        tile_end = jnp.cumsum(tiles_per)
        tile_start = tile_end - tiles_per
        bucket = ri[0]
        pos = (jnp.take(tile_start, bucket) * TMX + ri[1]).reshape(1, n)
        tile_bucket = jnp.minimum(jnp.searchsorted(tile_end, jnp.arange(ntile, dtype=I32), side="right"),
                                  N_BUCKETS - 1).astype(I32)
        tile_a = jnp.take(lo_tab, tile_bucket)
        tile_b = jnp.take(hi_tab, tile_bucket)
        nvalid = tile_end[-1:].astype(I32)

        xs_sorted = _dispatch(pos, ht, xs_sorted, ns)
        ys = _experts(tile_a, tile_b, nvalid, xs_sorted, w1_all, w3_all, w2_all, l, d)
        final = l == depth - 1
        if final:
            xcur, out = _combine(pos, ys, xmid, wcol, modsel[l], modsel[l], final_g, nct, True)
        else:
            xcur, h = _combine(pos, ys, xmid, wcol, modsel[l], modsel[l + 1], norm1_g[l + 1], nct, False)
    return out
```

```python
import functools
import math

import jax
import jax.numpy as jnp
import numpy as np
from jax import lax
from jax.experimental import pallas as pl
from jax.experimental.pallas import tpu as pltpu

F32 = jnp.float32
BF16 = jnp.bfloat16
I32 = jnp.int32
MXU_DTYPE = BF16

EPS = 1e-6
LOG2E = 1.4426950408889634
LN2 = 0.6931471805599453
GRID_W = 64
N_MOD = 6

GLA_HEADS = 4
GLA_DK = 128
GLA_DV = 256
GLA_RANK = 16
GLA_TAU = 16.0
GLA_CHUNK = 64
GLA_HPS = 2
GLA_BLOCK = 256

FNET_GROUPS = 4

SSD_HEADDIM = 64
SSD_GROUPS = 4
SSD_STATE = 128
SSD_CONV = 5
CONV_EDGE = 16
SSD_CHUNK = 128
HEADS_PER_GROUP = 8

N_EXPERTS = 16
N_EXPERT_GROUPS = 4
EXPERTS_PER_GROUP = 4
D_FF = 512
N_PAIRS = 6
N_BUCKETS = N_EXPERT_GROUPS * N_PAIRS
PAIR_LO = (0, 0, 0, 1, 1, 2)
PAIR_HI = (1, 2, 3, 2, 3, 3)

TM = 256
TMX = 256
TMD = 512
LANES = 128
SUB = 8
VMEM_LIMIT = 56 * 1024 * 1024

COL_Q, COL_K, COL_V, COL_R, COL_U, COL_Z, COL_X, COL_B, COL_C, COL_GA, COL_GB, COL_GC = (
    0, 512, 1024, 2048, 3072, 4096, 6144, 8192, 8704, 9216, 10240, 11264)
P_MAIN = 12288
P_SMALL = LANES * (1 + SSD_GROUPS)
IN_TN = 1024


def _cparams(sem, vmem=VMEM_LIMIT):
    return pltpu.CompilerParams(dimension_semantics=sem, vmem_limit_bytes=vmem)


def _mm(a, b):
    return jnp.dot(a.astype(MXU_DTYPE), b.astype(MXU_DTYPE), preferred_element_type=F32)


def _mm_nt(a, b):
    return lax.dot_general(a.astype(MXU_DTYPE), b.astype(MXU_DTYPE), (((1,), (1,)), ((), ())),
                           preferred_element_type=F32)


def _mm_tn(a, b):
    return lax.dot_general(a.astype(MXU_DTYPE), b.astype(MXU_DTYPE), (((0,), (0,)), ((), ())),
                           preferred_element_type=F32)


def _split3(x):
    hi = x.astype(MXU_DTYPE)
    r1 = x - hi.astype(F32)
    mid = r1.astype(MXU_DTYPE)
    lo = (r1 - mid.astype(F32)).astype(MXU_DTYPE)
    return hi, mid, lo


def _mm_exact_lhs(m01, x):
    hi = x.astype(MXU_DTYPE)
    lo = (x - hi.astype(F32)).astype(MXU_DTYPE)
    m = m01.astype(MXU_DTYPE)
    return jnp.dot(m, hi, preferred_element_type=F32) + jnp.dot(m, lo, preferred_element_type=F32)


def _sigmoid(x):
    return 1.0 / (1.0 + jnp.exp(-x))


def _silu(x):
    return x * _sigmoid(x)


def _softplus(x):
    return jnp.maximum(x, 0.0) + jnp.log(1.0 + jnp.exp(-jnp.abs(x)))


def _rms(x, g):
    return x * lax.rsqrt(jnp.mean(x * x, axis=-1, keepdims=True) + EPS) * g


def _modulate(x, g, shift, scale):
    return _rms(x, g) * (1.0 + scale) + shift


def _ada_kernel(c_ref, w_ref, b_ref, o_ref):
    o_ref[0] = jnp.dot(_silu(c_ref[...]), w_ref[0], preferred_element_type=F32) + b_ref[0]


def _ada(c_all, ada_w, ada_b):
    depth, d, nd = ada_w.shape
    rows = c_all.shape[0]
    tn = 1024
    return pl.pallas_call(
        _ada_kernel,
        grid=(depth, nd // tn),
        in_specs=[pl.BlockSpec((rows, d), lambda l, j: (0, 0)),
                  pl.BlockSpec((1, d, tn), lambda l, j: (l, 0, j)),
                  pl.BlockSpec((1, 1, tn), lambda l, j: (l, 0, j))],
        out_specs=pl.BlockSpec((1, rows, tn), lambda l, j: (l, 0, j)),
        out_shape=jax.ShapeDtypeStruct((depth, rows, nd), F32),
        compiler_params=_cparams(("arbitrary", "arbitrary")),
        name="ada_mod",
    )(c_all, ada_w, ada_b.reshape(depth, 1, nd))


def _normmod_kernel(x_ref, g_ref, mod_ref, h_ref):
    mod = mod_ref[0, 0]
    h_ref[0] = _modulate(x_ref[0], g_ref[...], mod[0:1], mod[1:2]).astype(h_ref.dtype)


def _normmod(x, g, modsel, n_ctx_tiles):
    b, s, d = x.shape
    return pl.pallas_call(
        _normmod_kernel,
        grid=(b, s // TM),
        in_specs=[pl.BlockSpec((1, TM, d), lambda i, j: (i, j, 0)),
                  pl.BlockSpec((1, d), lambda i, j: (0, 0)),
                  pl.BlockSpec((1, 1, SUB, d), lambda i, j: (i, jnp.where(j < n_ctx_tiles, 0, 1), 0, 0))],
        out_specs=pl.BlockSpec((1, TM, d), lambda i, j: (i, j, 0)),
        out_shape=jax.ShapeDtypeStruct((b, s, d), MXU_DTYPE),
        compiler_params=_cparams(("arbitrary", "arbitrary")),
        name="norm_mod",
    )(x, g.reshape(1, d), modsel)


def _inproj_kernel(h_ref, w_ref, ws_ref, p_ref, ps_ref):
    h = h_ref[0]
    p_ref[0] = _mm(h, w_ref[...]).astype(p_ref.dtype)

    @pl.when(pl.program_id(1) == 0)
    def _():
        ps_ref[0] = _mm(h, ws_ref[...])


def _inproj(h, w_main, w_small):
    b, s, d = h.shape
    return pl.pallas_call(
        _inproj_kernel,
        grid=(b, P_MAIN // IN_TN),
        in_specs=[pl.BlockSpec((1, s, d), lambda i, j: (i, 0, 0)),
                  pl.BlockSpec((d, IN_TN), lambda i, j: (0, j)),
                  pl.BlockSpec((d, P_SMALL), lambda i, j: (0, 0))],
        out_specs=[pl.BlockSpec((1, s, IN_TN), lambda i, j: (i, 0, j)),
                   pl.BlockSpec((1, s, P_SMALL), lambda i, j: (i, 0, 0))],
        out_shape=[jax.ShapeDtypeStruct((b, s, P_MAIN), MXU_DTYPE),
                   jax.ShapeDtypeStruct((b, s, P_SMALL), F32)],
        compiler_params=_cparams(("arbitrary", "arbitrary")),
        name="in_proj",
    )(h, w_main, w_small)


def _conv_kernel(p_ref, w_ref, b_ref, o_ref, *, n_ctx):
    x = p_ref[0].astype(F32)
    s = x.shape[0]
    w = w_ref[...]
    bias = b_ref[...]
    half = SSD_CONV // 2

    def taps(xw, row0, masked):
        r = xw.shape[0]
        acc = jnp.zeros_like(xw) + bias
        if masked:
            row = row0 + lax.broadcasted_iota(I32, (r, 1), 0)
            seg_lo = jnp.where(row < n_ctx, 0, n_ctx)
            seg_hi = jnp.where(row < n_ctx, n_ctx, s)
        for k in range(SSD_CONV):
            dlt = k - half
            xs = xw if dlt == 0 else pltpu.roll(xw, shift=(-dlt) % r, axis=0)
            if masked:
                t2 = row + dlt
                ok = jnp.broadcast_to(jnp.where((t2 >= seg_lo) & (t2 < seg_hi), 1.0, 0.0), xw.shape)
                xs = jnp.where(ok > 0.5, xs, 0.0)
            acc = acc + w[k:k + 1] * xs
        half_acc = 0.5 * acc
        return half_acc + half_acc * jnp.tanh(half_acc)

    o_ref[0] = taps(x, 0, False).astype(o_ref.dtype)
    eb, win = CONV_EDGE, 2 * CONV_EDGE
    for r0 in (0, n_ctx - eb, n_ctx, s - eb):
        a = min(max(r0 - SUB, 0), s - win)
        o_ref[0, r0:r0 + eb, :] = taps(x[a:a + win], a, True)[r0 - a:r0 - a + eb].astype(o_ref.dtype)


def _conv(p, conv_w, conv_b, n_ctx):
    b, s, _ = p.shape
    c = conv_w.shape[1]
    tc = 256
    return pl.pallas_call(
        functools.partial(_conv_kernel, n_ctx=n_ctx),
        grid=(b, c // tc),
        in_specs=[pl.BlockSpec((1, s, tc), lambda i, j: (i, 0, COL_X // tc + j)),
                  pl.BlockSpec((SSD_CONV, tc), lambda i, j: (0, j)),
                  pl.BlockSpec((1, tc), lambda i, j: (0, j))],
        out_specs=pl.BlockSpec((1, s, tc), lambda i, j: (i, 0, j)),
        out_shape=jax.ShapeDtypeStruct((b, s, c), MXU_DTYPE),
        compiler_params=_cparams(("arbitrary", "arbitrary")),
        name="ssd_conv",
    )(p, conv_w, conv_b.reshape(1, c))


def _gla_kernel(q_ref, k_ref, v_ref, r_ref, ps_ref, aw_ref, ab_ref, g_ref, o_ref,
                la_s, od_s, st_s, *, n_ctx):
    L = GLA_CHUNK
    R = GLA_BLOCK
    cpb = R // L
    dk, dv = GLA_DK, GLA_DV
    s = q_ref.shape[1]
    nb = s // R
    nbc = n_ctx // R
    lr = ps_ref[0]
    for d in range(2):
        pre = _mm(lr, aw_ref[d]) + ab_ref[d]
        la_s[d] = (jnp.minimum(pre, 0.0) * (1.0 / GLA_TAU)
                   - (LN2 / GLA_TAU) * jnp.log2(1.0 + jnp.exp2(jnp.abs(pre) * (-LOG2E))))

    ri = lax.broadcasted_iota(I32, (R, R), 0)
    ci = lax.broadcasted_iota(I32, (R, R), 1)
    same = (ri // L) == (ci // L)
    scale = dk ** -0.5
    st_s[...] = jnp.zeros_like(st_s)

    def per_chunk(x, idx):
        x3 = x.reshape(cpb, L, x.shape[-1])
        return jnp.broadcast_to(x3[:, idx:idx + 1, :], x3.shape).reshape(x.shape)

    def block(off, hh, reverse, st, keep, bcum_all):
        q = q_ref[0, pl.ds(off, R), hh * dk:(hh + 1) * dk].astype(F32) * scale
        k = k_ref[0, pl.ds(off, R), hh * dk:(hh + 1) * dk].astype(F32)
        v = v_ref[0, pl.ds(off, R), hh * dv:(hh + 1) * dv]
        bcum = bcum_all[:, hh * dk:(hh + 1) * dk]
        b_mid = per_chunk(bcum, L // 2 if reverse else L // 2 - 1)
        b_last = per_chunk(bcum, 0 if reverse else L - 1)
        sc = _mm_nt(q * jnp.exp(bcum - b_mid), k * jnp.exp(b_mid - bcum))
        o_intra = _mm(jnp.where(keep, sc, 0.0), v)
        qb = (q * jnp.exp(bcum)).astype(MXU_DTYPE)
        kd = (k * jnp.exp(b_last - bcum)).astype(MXU_DTYPE)
        dec = jnp.exp(b_last)
        o_inter = [None] * cpb
        for c in (range(cpb - 1, -1, -1) if reverse else range(cpb)):
            rows = slice(c * L, (c + 1) * L)
            o_inter[c] = _mm_nt(qb[rows], st)
            st = dec[c * L:c * L + 1] * st + _mm_tn(v[rows], kd[rows])
        return o_intra + jnp.concatenate(o_inter, axis=0), st

    @pl.loop(0, nb)
    def _(i):
        off_f = pl.multiple_of(i * R, R)
        jb = jnp.where(i < nbc, nbc - 1 - i, nb + nbc - 1 - i)
        off_b = pl.multiple_of(jb * R, R)
        offs = (off_f, off_b)
        keeps = (same & (ci <= ri), same & (ci >= ri))
        bcums = [_mm_exact_lhs(jnp.where(keeps[d], 1.0, 0.0), la_s[d, pl.ds(offs[d], R), :]) for d in range(2)]
        chains = [(d, hh) for hh in range(GLA_HPS) for d in range(2)]
        states = [st_s[d, hh] for d, hh in chains]
        res = [block(offs[d], hh, bool(d), st, keeps[d], bcums[d]) for (d, hh), st in zip(chains, states)]
        for (d, hh), (o, st) in zip(chains, res):
            od_s[d, pl.ds(off_b if d else off_f, R), hh * dv:(hh + 1) * dv] = o
            st_s[d, hh] = st

    g = g_ref[...]

    @pl.loop(0, nb)
    def _(c):
        off = pl.multiple_of(c * R, R)
        o = od_s[0, pl.ds(off, R), :] + od_s[1, pl.ds(off, R), :]
        r = r_ref[0, pl.ds(off, R), :].astype(F32)
        outs = [_rms(o[:, hh * dv:(hh + 1) * dv], g) for hh in range(GLA_HPS)]
        o_ref[0, pl.ds(off, R), :] = (jnp.concatenate(outs, axis=1) * _silu(r)).astype(o_ref.dtype)


def _gla(p, ps, aw_pad, ab, norm_g, n_ctx):
    b, s, _ = p.shape
    dk, dv = GLA_HPS * GLA_DK, GLA_HPS * GLA_DV
    return pl.pallas_call(
        functools.partial(_gla_kernel, n_ctx=n_ctx),
        grid=(b, GLA_HEADS // GLA_HPS),
        in_specs=[pl.BlockSpec((1, s, dk), lambda i, h: (i, 0, COL_Q // dk + h)),
                  pl.BlockSpec((1, s, dk), lambda i, h: (i, 0, COL_K // dk + h)),
                  pl.BlockSpec((1, s, dv), lambda i, h: (i, 0, COL_V // dv + h)),
                  pl.BlockSpec((1, s, dv), lambda i, h: (i, 0, COL_R // dv + h)),
                  pl.BlockSpec((1, s, LANES), lambda i, h: (i, 0, 0)),
                  pl.BlockSpec((2, LANES, dk), lambda i, h: (0, 0, h)),
                  pl.BlockSpec((2, 1, dk), lambda i, h: (0, 0, h)),
                  pl.BlockSpec((1, GLA_DV), lambda i, h: (0, 0))],
        out_specs=pl.BlockSpec((1, s, dv), lambda i, h: (i, 0, h)),
        out_shape=jax.ShapeDtypeStruct((b, s, GLA_HEADS * GLA_DV), MXU_DTYPE),
        scratch_shapes=[pltpu.VMEM((2, s, dk), F32), pltpu.VMEM((2, s, dv), F32),
                        pltpu.VMEM((2, GLA_HPS, GLA_DV, GLA_DK), F32)],
        compiler_params=_cparams(("arbitrary", "arbitrary")),
        name="gla",
    )(p, p, p, p, ps, aw_pad, ab, norm_g.reshape(1, GLA_DV))


def _ssd_kernel(z_ref, x_ref, bm_ref, cm_ref, ps_ref, brow_ref, alog_ref, d_ref, o_ref,
                dt_s, acs_s, rt_s, cb_s, y_s, st_s, *, n_ctx):
    L = SSD_CHUNK
    s = x_ref.shape[1]
    n = s // L
    nc = n_ctx // L
    hp = HEADS_PER_GROUP
    npair = hp // 2
    neg = -1e30
    dt_s[...] = _softplus(ps_ref[0] + brow_ref[0])
    a_row = -jnp.exp(alog_ref[0])

    ri = lax.broadcasted_iota(I32, (L, L), 0)
    ci = lax.broadcasted_iota(I32, (L, L), 1)
    lane = lax.broadcasted_iota(I32, (1, LANES), 1)
    low = lane < SSD_HEADDIM
    keeps = (ci <= ri, ci >= ri)
    er = lax.broadcasted_iota(I32, (LANES, hp * SSD_HEADDIM), 0)
    ec = lax.broadcasted_iota(I32, (LANES, hp * SSD_HEADDIM), 1) // SSD_HEADDIM
    esel = [jnp.where(er == d * hp + ec, 1.0, 0.0).astype(MXU_DTYPE) for d in range(2)]

    @pl.loop(0, n, unroll=2)
    def _(c):
        off = pl.multiple_of(c * L, L)
        dtc = dt_s[pl.ds(off, L), :]
        wc = dtc * a_row
        pre = _mm_exact_lhs(jnp.where(keeps[0], 1.0, 0.0), wc)
        suf = pre[L - 1:L] - pre + wc
        acs = jnp.where(lane < hp, pre, suf)
        acs_s[pl.ds(off, L), :] = acs * LOG2E
        rt_s[pl.ds(off, L), :] = ((acs - jnp.log(dtc)) * LOG2E).T
        cb_s[pl.ds(off, L), :] = _mm_nt(cm_ref[0, pl.ds(off, L), :], bm_ref[0, pl.ds(off, L), :])

    st_s[...] = jnp.zeros_like(st_s)

    def chunk(off, d, st):
        base = d * hp
        keep = keeps[d]
        i_last = 0 if d else L - 1
        xb = x_ref[0, pl.ds(off, L), :]
        bm = bm_ref[0, pl.ds(off, L), :]
        cm = cm_ref[0, pl.ds(off, L), :]
        dtc = dt_s[pl.ds(off, L), :]
        acs = acs_s[pl.ds(off, L), :]
        rt = rt_s[pl.ds(off, L), :]
        cb = cb_s[pl.ds(off, L), :]
        mine = (lane >= base) & (lane < base + hp)
        to_end = dtc * jnp.exp2(jnp.where(mine, acs[i_last:i_last + 1] - acs, 0.0))
        hi = to_end.astype(MXU_DTYPE)
        lo = (to_end - hi.astype(F32)).astype(MXU_DTYPE)
        to_end_e = (jnp.dot(hi, esel[d], preferred_element_type=F32)
                    + jnp.dot(lo, esel[d], preferred_element_type=F32))
        y_off = _mm(cm, st)
        ys, decs = [], []
        for pr in range(npair):
            a0 = acs[:, base + 2 * pr:base + 2 * pr + 1]
            a1 = acs[:, base + 2 * pr + 1:base + 2 * pr + 2]
            m2 = jnp.concatenate(
                [cb * jnp.exp2(jnp.where(keep, a0 - rt[base + 2 * pr:base + 2 * pr + 1, :], neg)),
                 cb * jnp.exp2(jnp.where(keep, a1 - rt[base + 2 * pr + 1:base + 2 * pr + 2, :], neg))], axis=1)
            xp = xb[:, pr * LANES:(pr + 1) * LANES]
            zero = jnp.zeros_like(xp)
            bd = jnp.concatenate([jnp.where(low, xp, zero), jnp.where(low, zero, xp)], axis=0)
            dec = jnp.exp2(jnp.where(low, a0, a1))
            ys.append(_mm(m2, bd) + y_off[:, pr * LANES:(pr + 1) * LANES] * dec)
            decs.append(dec[i_last:i_last + 1])
        st = jnp.concatenate(decs, axis=1) * st + _mm_tn(bm, xb.astype(F32) * to_end_e)
        return jnp.concatenate(ys, axis=1), st

    @pl.loop(0, n)
    def _(i):
        jb = jnp.where(i < nc, nc - 1 - i, n + nc - 1 - i)
        offs = (pl.multiple_of(i * L, L), pl.multiple_of(jb * L, L))
        states = [st_s[d] for d in range(2)]
        res = [chunk(offs[d], d, states[d]) for d in range(2)]
        for d, (y, st) in enumerate(res):
            y_s[d, pl.ds(offs[d], L), :] = y
            st_s[d] = st

    dsk = d_ref[0]

    @pl.loop(0, n)
    def _(c):
        off = pl.multiple_of(c * L, L)
        y = y_s[0, pl.ds(off, L), :] + y_s[1, pl.ds(off, L), :] + dsk * x_ref[0, pl.ds(off, L), :].astype(F32)
        z = z_ref[0, pl.ds(off, L), :].astype(F32)
        o_ref[0, pl.ds(off, L), :] = (y * _silu(z)).astype(o_ref.dtype)


def _ssd(p, xbc, ps, brow, alog_row, d_exp, n_ctx):
    b, s, _ = p.shape
    gw = HEADS_PER_GROUP * SSD_HEADDIM
    inner = SSD_GROUPS * gw
    ns = SSD_STATE
    return pl.pallas_call(
        functools.partial(_ssd_kernel, n_ctx=n_ctx),
        grid=(b, SSD_GROUPS),
        in_specs=[pl.BlockSpec((1, s, gw), lambda i, g: (i, 0, COL_Z // gw + g)),
                  pl.BlockSpec((1, s, gw), lambda i, g: (i, 0, g)),
                  pl.BlockSpec((1, s, ns), lambda i, g: (i, 0, inner // ns + g)),
                  pl.BlockSpec((1, s, ns), lambda i, g: (i, 0, inner // ns + SSD_GROUPS + g)),
                  pl.BlockSpec((1, s, LANES), lambda i, g: (i, 0, 1 + g)),
                  pl.BlockSpec((1, 1, LANES), lambda i, g: (g, 0, 0)),
                  pl.BlockSpec((1, 1, LANES), lambda i, g: (g, 0, 0)),
                  pl.BlockSpec((1, 1, gw), lambda i, g: (g, 0, 0))],
        out_specs=pl.BlockSpec((1, s, gw), lambda i, g: (i, 0, g)),
        out_shape=jax.ShapeDtypeStruct((b, s, inner), MXU_DTYPE),
        scratch_shapes=[pltpu.VMEM((s, LANES), F32), pltpu.VMEM((s, LANES), F32),
                        pltpu.VMEM((s, LANES), F32), pltpu.VMEM((s, LANES), F32),
                        pltpu.VMEM((2, s, gw), F32), pltpu.VMEM((2, ns, gw), F32)],
        compiler_params=_cparams(("arbitrary", "arbitrary")),
        name="ssd",
    )(p, xbc, xbc, xbc, ps, brow, alog_row, d_exp)


def _chan_dft_kernel(u_ref, w_ref, o_ref):
    u = u_ref[0]
    gd = w_ref.shape[0]
    w = w_ref[...]
    cs, sn = [], []
    for g in range(FNET_GROUPS):
        r = _mm(u[:, g * gd:(g + 1) * gd], w)
        cs.append(r[:, :gd])
        sn.append(r[:, gd:])
    o_ref[0] = jnp.concatenate(cs + sn, axis=1).astype(o_ref.dtype)


def _chan_dft(p, w_cs, width):
    b, s, _ = p.shape
    gd = width // FNET_GROUPS
    return pl.pallas_call(
        _chan_dft_kernel,
        grid=(b, s // TM),
        in_specs=[pl.BlockSpec((1, TM, width), lambda i, j: (i, j, COL_U // width)),
                  pl.BlockSpec((gd, 2 * gd), lambda i, j: (0, 0))],
        out_specs=pl.BlockSpec((1, TM, 2 * width), lambda i, j: (i, j, 0)),
        out_shape=jax.ShapeDtypeStruct((b, s, 2 * width), MXU_DTYPE),
        compiler_params=_cparams(("arbitrary", "arbitrary")),
        name="fnet_chan",
    )(p, w_cs)


def _pos_dft_ctx_kernel(c_ref, s_ref, uc_ref, us_ref, o_ref):
    o_ref[0] = (_mm(c_ref[...], uc_ref[0]) - _mm(s_ref[...], us_ref[0])).astype(o_ref.dtype)


def _pos_dft_ctx(v, cos_c, sin_c, width):
    b = v.shape[0]
    n_ctx = cos_c.shape[0]
    return pl.pallas_call(
        _pos_dft_ctx_kernel,
        grid=(b, n_ctx // TM),
        in_specs=[pl.BlockSpec((TM, n_ctx), lambda i, j: (j, 0)),
                  pl.BlockSpec((TM, n_ctx), lambda i, j: (j, 0)),
                  pl.BlockSpec((1, n_ctx, width), lambda i, j: (i, 0, 0)),
                  pl.BlockSpec((1, n_ctx, width), lambda i, j: (i, 0, 1))],
        out_specs=pl.BlockSpec((1, TM, width), lambda i, j: (i, j, 0)),
        out_shape=jax.ShapeDtypeStruct((b, n_ctx, width), MXU_DTYPE),
        compiler_params=_cparams(("arbitrary", "arbitrary")),
        name="fnet_pos_ctx",
    )(cos_c, sin_c, v, v)


def _pos_dft_lat_kernel(cw_ref, sw_ref, jr_ref, uc_ref, us_ref, o_ref, *, n_ctx):
    a = _mm(cw_ref[0], uc_ref[0, n_ctx:, :])
    bm = _mm(sw_ref[0], us_ref[0, n_ctx:, :])
    o_ref[0, :TM, :] = (a - bm)[:TM].astype(o_ref.dtype)
    o_ref[0, TM:, :] = _mm(jr_ref[...], a + bm).astype(o_ref.dtype)


def _pos_dft_lat(v, cos_w, sin_w, jrev, n_ctx, width):
    b, s, _ = v.shape
    nh, wr, t = cos_w.shape
    return pl.pallas_call(
        functools.partial(_pos_dft_lat_kernel, n_ctx=n_ctx),
        grid=(b, nh),
        in_specs=[pl.BlockSpec((1, wr, t), lambda i, j: (j, 0, 0)),
                  pl.BlockSpec((1, wr, t), lambda i, j: (j, 0, 0)),
                  pl.BlockSpec((TM, wr), lambda i, j: (0, 0)),
                  pl.BlockSpec((1, s, width), lambda i, j: (i, 0, 0)),
                  pl.BlockSpec((1, s, width), lambda i, j: (i, 0, 1))],
        out_specs=pl.BlockSpec((1, 2 * TM, width), lambda i, j: (i, j, 0)),
        out_shape=jax.ShapeDtypeStruct((b, t, width), MXU_DTYPE),
        compiler_params=_cparams(("arbitrary", "arbitrary")),
        name="fnet_pos",
    )(cos_w, sin_w, jrev, v, v)


def _merge_kernel(ga_ref, fmc_ref, fml_ref, yz_ref, g1_ref, g2_ref, g3_ref, x_ref, mod_ref, n2g_ref, sg_ref,
                  wa_ref, wb_ref, wc_ref, wo_ref, rw_ref, rb_ref,
                  xo_ref, ht_ref, ri_ref, rf_ref, wcol_ref, cnt_ref, carry_s, *, n_ctx_tiles):
    first = (pl.program_id(0) == 0) & (pl.program_id(1) == 0)

    @pl.when(first)
    def _():
        carry_s[...] = jnp.zeros_like(carry_s)

    mod = mod_ref[0, 0]
    ya = _mm(ga_ref[0], wa_ref[...])
    fm = jnp.where(pl.program_id(1) < n_ctx_tiles, fmc_ref[0], fml_ref[0])
    yb = _mm(fm, wb_ref[...])
    yz = yz_ref[0].astype(F32)
    yc = _mm(_rms(yz, sg_ref[...]), wc_ref[...])
    y = (_sigmoid(g1_ref[0].astype(F32)) * ya + _sigmoid(g2_ref[0].astype(F32)) * yb
         + _sigmoid(g3_ref[0].astype(F32)) * yc)
    xn = x_ref[0] + mod[2:3] * _mm(y, wo_ref[...])
    xo_ref[0] = xn
    h2 = _modulate(xn, n2g_ref[...], mod[3:4], mod[4:5])
    d = h2.shape[1]
    tm = h2.shape[0]
    for sidx in range(d // LANES):
        ht_ref[pl.ds(sidx, tm, stride=d // LANES), :] = h2[:, sidx * LANES:(sidx + 1) * LANES]

    hh, hm, _hl = _split3(h2)
    rh, rm, _rl = _split3(rw_ref[...])
    nt = (((1,), (1,)), ((), ()))
    lg = (lax.dot_general(rh, hh, nt, preferred_element_type=F32)
          + lax.dot_general(rh, hm, nt, preferred_element_type=F32)
          + lax.dot_general(rm, hh, nt, preferred_element_type=F32))
    sc = _sigmoid(lg)
    sel = sc + rb_ref[...]
    srow = [sel[e:e + 1] for e in range(N_EXPERTS)]
    urow = [sc[e:e + 1] for e in range(N_EXPERTS)]
    epg = EXPERTS_PER_GROUP

    best_v = best_pair = best_g = best_slo = best_shi = None
    for g in range(N_EXPERT_GROUPS):
        a = srow[g * epg:(g + 1) * epg]
        u = urow[g * epg:(g + 1) * epg]
        gs = None
        for i in range(epg):
            for j in range(i + 1, epg):
                ps_ = a[i] + a[j]
                gs = ps_ if gs is None else jnp.maximum(gs, ps_)
        m1, i1 = a[0], jnp.zeros_like(a[0], dtype=I32)
        for i in range(1, epg):
            up = a[i] > m1
            i1 = jnp.where(up, i, i1)
            m1 = jnp.where(up, a[i], m1)
        m2 = jnp.full_like(a[0], -jnp.inf)
        i2 = jnp.full_like(i1, -1)
        for i in range(epg):
            cand = (i1 != i) & (a[i] > m2)
            i2 = jnp.where(cand, i, i2)
            m2 = jnp.where(cand, a[i], m2)
        lo = jnp.minimum(i1, i2)
        hi = jnp.maximum(i1, i2)
        pair = jnp.where(lo == 0, 0, jnp.where(lo == 1, 3, 5)) + (hi - lo - 1)
        s_lo = jnp.where(lo == 0, u[0], jnp.where(lo == 1, u[1], u[2]))
        s_hi = jnp.where(hi == 1, u[1], jnp.where(hi == 2, u[2], u[3]))
        if g == 0:
            best_v, best_pair, best_g, best_slo, best_shi = gs, pair, jnp.zeros_like(pair), s_lo, s_hi
        else:
            up = gs > best_v
            best_v = jnp.where(up, gs, best_v)
            best_pair = jnp.where(up, pair, best_pair)
            best_g = jnp.where(up, g, best_g)
            best_slo = jnp.where(up, s_lo, best_slo)
            best_shi = jnp.where(up, s_hi, best_shi)
    bucket = best_g * N_PAIRS + best_pair
    tot = best_slo + best_shi
    w_lo = best_slo / tot
    w_hi = best_shi / tot

    nb = carry_s.shape[0]
    oh = (lax.broadcasted_iota(I32, (nb, tm), 0) == bucket).astype(F32)
    tri = (lax.broadcasted_iota(I32, (tm, tm), 0) <= lax.broadcasted_iota(I32, (tm, tm), 1))
    csum = _mm(oh, jnp.where(tri, 1.0, 0.0))
    carry = carry_s[...]
    rank = jnp.sum(oh * (csum - 1.0 + carry), axis=0, keepdims=True)
    carry = carry + jnp.sum(oh, axis=1, keepdims=True)
    carry_s[...] = carry
    cnt_ref[...] = jnp.broadcast_to(carry, cnt_ref.shape).astype(I32)

    zi = jnp.zeros((SUB - 2, tm), I32)
    ri_ref[...] = jnp.concatenate([bucket, rank.astype(I32), zi], axis=0)
    wrows = jnp.concatenate([w_lo, w_hi, jnp.zeros((SUB - 2, tm), F32)], axis=0)
    rf_ref[...] = wrows
    eye = (lax.broadcasted_iota(I32, (SUB, LANES), 0) == lax.broadcasted_iota(I32, (SUB, LANES), 1))
    eye = jnp.where(eye, 1.0, 0.0).astype(MXU_DTYPE)
    tn = (((0,), (0,)), ((), ()))
    p1, p2, p3 = _split3(wrows)
    wcol_ref[...] = (lax.dot_general(p1, eye, tn, preferred_element_type=F32)
                     + lax.dot_general(p2, eye, tn, preferred_element_type=F32)
                     + lax.dot_general(p3, eye, tn, preferred_element_type=F32))


def _merge(ga, fmc, fml, yz, p, x, modsel, n2g, ssd_g, w_a, w_b, w_c, w_o, rw_t, rb, n_ctx_tiles):
    b, s, d = x.shape
    nt = s // TM
    n = b * s
    di = yz.shape[2]
    nbp = 32
    tok = lambda i, j: (i, j, 0)
    flat = lambda i, j: (0, i * nt + j)
    const = lambda i, j: (0, 0)
    nct = n_ctx_tiles
    nh = (nt - nct) // 2

    def fml_idx(i, j):
        m = jnp.maximum(j - nct, 0)
        return (i, jnp.where(m < nh, 2 * m, 2 * (2 * nh - 1 - m) + 1), 0)

    return pl.pallas_call(
        functools.partial(_merge_kernel, n_ctx_tiles=nct),
        grid=(b, nt),
        in_specs=[pl.BlockSpec((1, TM, d), tok),
                  pl.BlockSpec((1, TM, d), lambda i, j: (i, jnp.minimum(j, nct - 1), 0)),
                  pl.BlockSpec((1, TM, d), fml_idx),
                  pl.BlockSpec((1, TM, di), tok),
                  pl.BlockSpec((1, TM, d), lambda i, j: (i, j, COL_GA // d)),
                  pl.BlockSpec((1, TM, d), lambda i, j: (i, j, COL_GB // d)),
                  pl.BlockSpec((1, TM, d), lambda i, j: (i, j, COL_GC // d)),
                  pl.BlockSpec((1, TM, d), tok),
                  pl.BlockSpec((1, 1, SUB, d), lambda i, j: (i, jnp.where(j < n_ctx_tiles, 0, 1), 0, 0)),
                  pl.BlockSpec((1, d), const),
                  pl.BlockSpec((1, di), const),
                  pl.BlockSpec((d, d), const),
                  pl.BlockSpec((d, d), const),
                  pl.BlockSpec((di, d), const),
                  pl.BlockSpec((d, d), const),
                  pl.BlockSpec((N_EXPERTS, d), const),
                  pl.BlockSpec((N_EXPERTS, 1), const)],
        out_specs=[pl.BlockSpec((1, TM, d), tok),
                   pl.BlockSpec((TM * (d // LANES), LANES), lambda i, j: (i * nt + j, 0)),
                   pl.BlockSpec((SUB, TM), flat),
                   pl.BlockSpec((SUB, TM), flat),
                   pl.BlockSpec((TM, LANES), lambda i, j: (i * nt + j, 0)),
                   pl.BlockSpec((nbp, LANES), const)],
        out_shape=[jax.ShapeDtypeStruct((b, s, d), F32),
                   jax.ShapeDtypeStruct((n * (d // LANES), LANES), F32),
                   jax.ShapeDtypeStruct((SUB, n), I32),
                   jax.ShapeDtypeStruct((SUB, n), F32),
                   jax.ShapeDtypeStruct((n, LANES), F32),
                   jax.ShapeDtypeStruct((nbp, LANES), I32)],
        scratch_shapes=[pltpu.VMEM((nbp, 1), F32)],
        compiler_params=_cparams(("arbitrary", "arbitrary")),
        name="merge_route",
    )(ga, fmc, fml, yz, p, p, p, x, modsel, n2g, ssd_g, w_a, w_b, w_c, w_o, rw_t, rb)


def _row_copy(src, dst, src_row, dst_row, rows, sem):
    def at(ref, r):
        start = r * rows if isinstance(r, int) else pl.multiple_of(r * rows, rows)
        return ref.at[pl.ds(start, rows)]
    return pltpu.make_async_copy(at(src, src_row), at(dst, dst_row), sem)


def _dispatch_kernel(pos_ref, ht_ref, xs_in_ref, xs_ref, sem, *, rows):
    del xs_in_ref
    copies = [_row_copy(ht_ref, xs_ref, t, pos_ref[0, t], rows, sem) for t in range(TMD)]
    for cp in copies:
        cp.start()
    for cp in copies:
        cp.wait()


def _dispatch(pos, ht, xs_init, rows):
    n = pos.shape[1]
    return pl.pallas_call(
        functools.partial(_dispatch_kernel, rows=rows),
        grid=(n // TMD,),
        in_specs=[pl.BlockSpec((1, TMD), lambda i: (0, i), memory_space=pltpu.SMEM),
                  pl.BlockSpec((TMD * rows, LANES), lambda i: (i, 0)),
                  pl.BlockSpec(memory_space=pl.ANY)],
        out_specs=pl.BlockSpec(memory_space=pl.ANY),
        out_shape=jax.ShapeDtypeStruct(xs_init.shape, xs_init.dtype),
        input_output_aliases={2: 0},
        scratch_shapes=[pltpu.SemaphoreType.DMA(())],
        compiler_params=_cparams(("arbitrary",)),
        name="moe_dispatch",
    )(pos, ht, xs_init)


def _expert_kernel(ta_ref, tb_ref, nv_ref, xs_ref, w1a_ref, w3a_ref, w2a_ref, w1b_ref, w3b_ref, w2b_ref, ys_ref):
    del ta_ref, tb_ref

    @pl.when(pl.program_id(0) >= nv_ref[0])
    def _():
        ys_ref[...] = jnp.zeros_like(ys_ref)

    @pl.when(pl.program_id(0) < nv_ref[0])
    def _():
        ns = xs_ref.shape[0] // TMX
        x = jnp.concatenate([xs_ref[pl.ds(sidx, TMX, stride=ns), :] for sidx in range(ns)], axis=1)
        x = x.astype(MXU_DTYPE)
        for kk, (w1_ref, w3_ref, w2_ref) in enumerate(((w1a_ref, w3a_ref, w2a_ref), (w1b_ref, w3b_ref, w2b_ref))):
            he = _silu(_mm(x, w1_ref[0, 0])) * _mm(x, w3_ref[0, 0])
            y = _mm(he, w2_ref[0, 0])
            for sidx in range(ns):
                ys_ref[pl.ds(kk * ns + sidx, TMX, stride=2 * ns), :] = y[:, sidx * LANES:(sidx + 1) * LANES]


def _experts(tile_a, tile_b, nvalid, xs, w1, w3, w2, layer, d):
    ns = d // LANES
    ntile = xs.shape[0] // (TMX * ns)
    last = lambda i, nv: jnp.minimum(i, nv[0] - 1)
    up = lambda t: pl.BlockSpec((1, 1, d, D_FF), lambda i, ta, tb, nv: (layer, (ta, tb)[t][last(i, nv)], 0, 0))
    down = lambda t: pl.BlockSpec((1, 1, D_FF, d), lambda i, ta, tb, nv: (layer, (ta, tb)[t][last(i, nv)], 0, 0))
    grid_spec = pltpu.PrefetchScalarGridSpec(
        num_scalar_prefetch=3,
        grid=(ntile,),
        in_specs=[pl.BlockSpec((TMX * ns, LANES), lambda i, ta, tb, nv: (last(i, nv), 0)),
                  up(0), up(0), down(0), up(1), up(1), down(1)],
        out_specs=pl.BlockSpec((TMX * 2 * ns, LANES), lambda i, ta, tb, nv: (i, 0)),
    )
    return pl.pallas_call(
        _expert_kernel,
        grid_spec=grid_spec,
        out_shape=jax.ShapeDtypeStruct((ntile * TMX * 2 * ns, LANES), F32),
        compiler_params=_cparams(("arbitrary",)),
        name="moe_experts",
    )(tile_a, tile_b, nvalid, xs, w1, w3, w2, w1, w3, w2)


def _combine_kernel(pos_ref, posn_ref, ys_ref, x_ref, wcol_ref, mod_ref, modn_ref, g_ref, xo_ref, ho_ref, buf, sem,
                    *, rows, final):
    step = pl.program_id(0) * pl.num_programs(1) + pl.program_id(1)
    last = pl.num_programs(0) * pl.num_programs(1) - 1
    slot = step % 2

    def gathers(p_ref, sl):
        return [_row_copy(ys_ref, buf.at[sl], p_ref[0, t], t, rows, sem.at[sl]) for t in range(TM)]

    @pl.when(step == 0)
    def _():
        for cp in gathers(pos_ref, 0):
            cp.start()

    @pl.when(step < last)
    def _():
        for cp in gathers(posn_ref, 1 - slot):
            cp.start()

    for cp in gathers(pos_ref, slot):
        cp.wait()

    ns = rows // 2
    ya = jnp.concatenate([buf[slot, pl.ds(sidx, TM, stride=rows), :] for sidx in range(ns)], axis=1)
    yb = jnp.concatenate([buf[slot, pl.ds(ns + sidx, TM, stride=rows), :] for sidx in range(ns)], axis=1)
    wc = wcol_ref[...]
    y = wc[:, 0:1] * ya + wc[:, 1:2] * yb
    mod = mod_ref[0, 0]
    xn = x_ref[0] + mod[5:6] * y
    xo_ref[0] = xn
    if final:
        ho_ref[0] = _rms(xn, g_ref[...]).astype(ho_ref.dtype)
    else:
        modn = modn_ref[0, 0]
        ho_ref[0] = _modulate(xn, g_ref[...], modn[0:1], modn[1:2]).astype(ho_ref.dtype)


def _combine(pos, ys, x, wcol, modsel, modsel_next, g_next, n_ctx_tiles, final):
    b, s, d = x.shape
    nt = s // TM
    rows = 2 * (d // LANES)
    tok = lambda i, j: (i, j, 0)
    msel = lambda i, j: (i, jnp.where(j < n_ctx_tiles, 0, 1), 0, 0)
    if final:
        t_lat = s - n_ctx_tiles * TM
        h_shape = jax.ShapeDtypeStruct((b, t_lat, d), F32)
        h_spec = pl.BlockSpec((1, TM, d), lambda i, j: (i, jnp.maximum(j - n_ctx_tiles, 0), 0))
    else:
        h_shape = jax.ShapeDtypeStruct((b, s, d), MXU_DTYPE)
        h_spec = pl.BlockSpec((1, TM, d), tok)
    return pl.pallas_call(
        functools.partial(_combine_kernel, rows=rows, final=final),
        grid=(b, nt),
        in_specs=[pl.BlockSpec((1, TM), lambda i, j: (0, i * nt + j), memory_space=pltpu.SMEM),
                  pl.BlockSpec((1, TM), lambda i, j: (0, jnp.minimum(i * nt + j + 1, b * nt - 1)),
                               memory_space=pltpu.SMEM),
                  pl.BlockSpec(memory_space=pl.ANY),
                  pl.BlockSpec((1, TM, d), tok),
                  pl.BlockSpec((TM, LANES), lambda i, j: (i * nt + j, 0)),
                  pl.BlockSpec((1, 1, SUB, d), msel),
                  pl.BlockSpec((1, 1, SUB, d), msel),
                  pl.BlockSpec((1, d), lambda i, j: (0, 0))],
        out_specs=[pl.BlockSpec((1, TM, d), tok), h_spec],
        out_shape=[jax.ShapeDtypeStruct((b, s, d), F32), h_shape],
        scratch_shapes=[pltpu.VMEM((2, TM * rows, LANES), F32), pltpu.SemaphoreType.DMA((2,))],
        compiler_params=_cparams(("arbitrary", "arbitrary")),
        name="moe_combine_final" if final else "moe_combine",
    )(pos, pos, ys, x, wcol, modsel, modsel_next, g_next.reshape(1, d))


def _grid_sincos(n_tokens, dim):
    rows = n_tokens // GRID_W
    row = np.broadcast_to(np.arange(rows)[:, None], (rows, GRID_W)).reshape(-1).astype(np.float32)
    col = np.broadcast_to(np.arange(GRID_W)[None, :], (rows, GRID_W)).reshape(-1).astype(np.float32)
    quarter = dim // 4
    freqs = np.exp(np.float32(-math.log(10000.0)) * np.arange(quarter, dtype=np.float32) / np.float32(quarter))
    ar = row[:, None] * freqs
    ac = col[:, None] * freqs
    return np.concatenate([np.sin(ar), np.cos(ar), np.sin(ac), np.cos(ac)], axis=-1).astype(np.float32)


def _dft_mats(n, scale, rows=None):
    k = np.arange(n) if rows is None else np.asarray(rows)
    ang = ((k[:, None] * np.arange(n)[None, :]) % n) * (2.0 * math.pi / n)
    return (np.cos(ang) * scale).astype(np.float32), (np.sin(ang) * scale).astype(np.float32)


def _split_cols(w, sizes):
    out, start = [], 0
    for sz in sizes:
        out.append(w[..., start:start + sz])
        start += sz
    return out


def _layer_weights(d, w_in_l, gla_a_w_l, gla_a_b_l, dt_bias_l, a_log_l, ssd_d_l):
    qk = GLA_HEADS * GLA_DK
    vv = GLA_HEADS * GLA_DV
    inner = SSD_GROUPS * HEADS_PER_GROUP * SSD_HEADDIM
    gn = SSD_GROUPS * SSD_STATE
    nh = SSD_GROUPS * HEADS_PER_GROUP
    sizes = (qk, qk, vv, vv, GLA_RANK, GLA_RANK, d, inner, inner + 2 * gn, nh, nh, d, d, d)
    (wq, wk, wv, wr, wlf, wlb, wu, wz, wxbc, wdf, wdb, wga, wgb, wgc) = _split_cols(w_in_l, sizes)
    w_main = jnp.concatenate([wq, wk, wv, wr, wu, wz, wxbc, wga, wgb, wgc], axis=1).astype(MXU_DTYPE)
    hp = HEADS_PER_GROUP
    pad = lambda w, width: jnp.pad(w, ((0, 0), (0, width - w.shape[1])))
    small = [pad(jnp.concatenate([wlf, wlb], axis=1), LANES)]
    for g in range(SSD_GROUPS):
        small.append(pad(jnp.concatenate([wdf[:, g * hp:(g + 1) * hp], wdb[:, g * hp:(g + 1) * hp]], axis=1), LANES))
    w_small = jnp.concatenate(small, axis=1).astype(MXU_DTYPE)
    aw_pad = jnp.zeros((2, LANES, qk), F32)
    aw_pad = aw_pad.at[0, :GLA_RANK].set(gla_a_w_l[0]).at[1, GLA_RANK:2 * GLA_RANK].set(gla_a_w_l[1])
    aw_pad = aw_pad.astype(MXU_DTYPE)
    ab = gla_a_b_l.reshape(2, 1, qk)
    grp = lambda v: v.reshape(SSD_GROUPS, hp)
    brow = jnp.concatenate([grp(dt_bias_l[0]), grp(dt_bias_l[1]), jnp.zeros((SSD_GROUPS, LANES - 2 * hp), F32)], axis=1)
    arow = jnp.concatenate([grp(a_log_l[0]), grp(a_log_l[1]), jnp.zeros((SSD_GROUPS, LANES - 2 * hp), F32)], axis=1)
    d_exp = jnp.repeat(ssd_d_l, SSD_HEADDIM).reshape(SSD_GROUPS, 1, hp * SSD_HEADDIM)
    return w_main, w_small, aw_pad, ab, brow.reshape(SSD_GROUPS, 1, LANES), arow.reshape(SSD_GROUPS, 1, LANES), d_exp


def kernel(x, c, ctx, c_ctx, ada_w, ada_b, norm1_g, norm2_g, w_in, gla_a_w, gla_a_b, gla_norm_g, gla_proj,
           fnet_proj, ssd_conv_w, ssd_conv_b, ssd_dt_bias, ssd_a_log, ssd_d, ssd_norm_g, ssd_proj, w_out,
           router_w, router_b, exp_w1, exp_w3, exp_w2, final_g):
    bsz, t_lat, d = x.shape
    n_ctx = ctx.shape[1]
    depth = ada_w.shape[0]
    s = n_ctx + t_lat
    n = bsz * s
    nct = n_ctx // TM
    ns = d // LANES

    xs0 = jnp.concatenate([ctx, x + _grid_sincos(t_lat, d)], axis=1)

    rows = 16
    c_all = jnp.concatenate([c, c_ctx[None], jnp.zeros((rows - bsz - 1, d), F32)], axis=0)
    mods = _ada(c_all, ada_w, ada_b)
    mods = mods.reshape(depth, rows, N_MOD, d)
    mods = jnp.pad(mods, ((0, 0), (0, 0), (0, SUB - N_MOD), (0, 0)))
    modsel = jnp.stack([jnp.broadcast_to(mods[:, bsz:bsz + 1], (depth, bsz, SUB, d)), mods[:, :bsz]], axis=2)

    width = d
    gd = width // FNET_GROUPS
    w_cs = jnp.asarray(np.concatenate(_dft_mats(gd, gd ** -0.5), axis=1), MXU_DTYPE)
    cos_c, sin_c = (jnp.asarray(m, MXU_DTYPE) for m in _dft_mats(n_ctx, n_ctx ** -0.5))
    n_win = t_lat // (2 * TM)
    win_rows = (np.arange(n_win)[:, None] * TM + np.arange(TM + SUB)[None, :]).reshape(-1)
    cos_w, sin_w = (jnp.asarray(m.reshape(n_win, TM + SUB, t_lat), MXU_DTYPE)
                    for m in _dft_mats(t_lat, t_lat ** -0.5, win_rows))
    jrev = jnp.asarray(np.arange(TM + SUB)[None, :] == TM - np.arange(TM)[:, None], MXU_DTYPE)

    rw_t = router_w.T
    rb = router_b.reshape(N_EXPERTS, 1)
    lo_tab = jnp.array([4 * (bk // N_PAIRS) + PAIR_LO[bk % N_PAIRS] for bk in range(N_BUCKETS)], I32)
    hi_tab = jnp.array([4 * (bk // N_PAIRS) + PAIR_HI[bk % N_PAIRS] for bk in range(N_BUCKETS)], I32)
    cap_rows = n + N_BUCKETS * TMX
    ntile = cap_rows // TMX

    w1_all, w3_all, w2_all = (w.astype(MXU_DTYPE) for w in (exp_w1, exp_w3, exp_w2))
    xs_sorted = jnp.zeros((cap_rows * ns, LANES), F32)
    xcur = xs0
    h = _normmod(xcur, norm1_g[0], modsel[0], nct)
    out = None
    for l in range(depth):
        w_main, w_small, aw_pad, ab, brow, arow, d_exp = _layer_weights(
            d, w_in[l], gla_a_w[l], gla_a_b[l], ssd_dt_bias[l], ssd_a_log[l], ssd_d[l])
        p, ps = _inproj(h, w_main, w_small)
        xbc = _conv(p, ssd_conv_w[l], ssd_conv_b[l], n_ctx)
        ga = _gla(p, ps, aw_pad, ab, gla_norm_g[l], n_ctx)
        yz = _ssd(p, xbc, ps, brow, arow, d_exp, n_ctx)
        v = _chan_dft(p, w_cs, width)
        fmc = _pos_dft_ctx(v, cos_c, sin_c, width)
        fml = _pos_dft_lat(v, cos_w, sin_w, jrev, n_ctx, width)
        xmid, ht, ri, rf, wcol, cnt = _merge(
            ga, fmc, fml, yz, p, xcur, modsel[l], norm2_g[l].reshape(1, d), ssd_norm_g[l].reshape(1, -1),
            gla_proj[l].astype(MXU_DTYPE), fnet_proj[l].astype(MXU_DTYPE), ssd_proj[l].astype(MXU_DTYPE),
            w_out[l].astype(MXU_DTYPE), rw_t, rb, nct)

        counts = cnt[:N_BUCKETS, 0]
        tiles_per = (counts + TMX - 1) // TMX
        tile_end = jnp.cumsum(tiles_per)
        tile_start = tile_end - tiles_per
        bucket = ri[0]
        pos = (jnp.take(tile_start, bucket) * TMX + ri[1]).reshape(1, n)
        tile_bucket = jnp.minimum(jnp.searchsorted(tile_end, jnp.arange(ntile, dtype=I32), side="right"),
                                  N_BUCKETS - 1).astype(I32)
        tile_a = jnp.take(lo_tab, tile_bucket)
        tile_b = jnp.take(hi_tab, tile_bucket)
        nvalid = tile_end[-1:].astype(I32)

        xs_sorted = _dispatch(pos, ht, xs_sorted, ns)
        ys = _experts(tile_a, tile_b, nvalid, xs_sorted, w1_all, w3_all, w2_all, l, d)
        final = l == depth - 1
        if final:
            xcur, out = _combine(pos, ys, xmid, wcol, modsel[l], modsel[l], final_g, nct, True)
        else:
            xcur, h = _combine(pos, ys, xmid, wcol, modsel[l], modsel[l + 1], norm1_g[l + 1], nct, False)
    return out
```
